```python
import math
import jax, jax.numpy as jnp
from jax import lax
import numpy as np

D_MODEL = 1024
BATCH = 2
SEQ = 8192
DEPTH = 4
DEC_BATCH = 32
DEC_SEQ = 8
PAST_LEN = 8192
PAGE_SIZE = 128

N_MIXERS = 2
N_ATTN_LAYERS = (DEPTH + 1) // 2
N_RNN_LAYERS = DEPTH // 2
WINDOWS = (128, 512, 2048)
DILATIONS = (1, 4, 16)
N_GROUPS = 3
HEADS_PER_GROUP = 8
HEAD_DIM = 128
N_HEADS = N_GROUPS * HEADS_PER_GROUP
QKV_WIDTH = N_HEADS * HEAD_DIM
ATTN_WIDTH = HEADS_PER_GROUP * HEAD_DIM
Q_BLK = 128
N_BUCKETS = 32
MAX_DISTANCE = 2048
D_RNN = 1280
RNN_BLOCKS = 10
RNN_BLOCK_W = D_RNN // RNN_BLOCKS
CONV_W = 4
LRU_C = 8.0
EPS = 1e-6
NEG = -1e30

kernel_name = 'hybrid_dilated_attn_rglru_step'


def rms_norm(x, g):
    xf = x.astype(jnp.float32)
    y = xf * lax.rsqrt(jnp.mean(xf * xf, axis=-1, keepdims=True) + EPS)
    return (y * g.astype(jnp.float32)).astype(x.dtype)


def t5_bucket(dist):
    n = jnp.maximum(dist, 0)
    max_exact = N_BUCKETS // 2
    nf = jnp.maximum(n, 1).astype(jnp.float32)
    large = max_exact + (jnp.log(nf / max_exact) / math.log(MAX_DISTANCE / max_exact)
                         * (N_BUCKETS - max_exact)).astype(jnp.int32)
    large = jnp.minimum(large, N_BUCKETS - 1)
    return jnp.where(n < max_exact, n, large)


def attn_project(x, norm_g, w_in, q_g, k_g):
    h = rms_norm(x, norm_g)
    proj = h @ w_in
    q, k, v, gate = jnp.split(proj, [QKV_WIDTH, 2 * QKV_WIDTH, 3 * QKV_WIDTH], axis=-1)
    shp = x.shape[:2] + (N_HEADS, HEAD_DIM)
    q = rms_norm(q.reshape(shp), q_g)
    k = rms_norm(k.reshape(shp), k_g)
    return q, k, v.reshape(shp), gate


def dilated_band_prompt(q, k, v, bias_tab, dil, n_keys):
    B, S, H, Dh = q.shape
    span = dil * Q_BLK
    Sp = -(-S // span) * span
    L = Sp // dil
    nb = L // Q_BLK

    def to_blocks(t):
        t = jnp.pad(t, ((0, 0), (0, Sp - S), (0, 0), (0, 0)))
        t = t.reshape(B, L, dil, H, Dh).transpose(0, 2, 1, 3, 4)
        return t.reshape(B, dil, nb, Q_BLK, H, Dh)

    def band(t):
        prev = jnp.pad(t, ((0, 0), (0, 0), (1, 0), (0, 0), (0, 0), (0, 0)))[:, :, :-1]
        return jnp.concatenate([prev, t], axis=3)

    qb = to_blocks(q)
    kk = band(to_blocks(k))
    vv = band(to_blocks(v))
    a = jnp.arange(Q_BLK)[:, None]
    bcol = jnp.arange(2 * Q_BLK)[None, :]
    step = a + Q_BLK - bcol
    in_band = (step >= 0) & (step <= n_keys)
    bias = bias_tab[t5_bucket(step * dil)]
    bias = jnp.where(in_band[..., None], bias, NEG).transpose(2, 0, 1)
    exists = (jnp.arange(nb)[:, None] > 0) | (bcol >= Q_BLK)
    s = jnp.einsum('brnqhd,brnkhd->brnhqk', qb, kk) * (HEAD_DIM ** -0.5) + bias
    s = jnp.where(exists[:, None, None, :], s, NEG)
    lse = jax.nn.logsumexp(s, axis=-1)
    p = jnp.exp(s - lse[..., None])
    o = jnp.einsum('brnhqk,brnkhd->brnqhd', p, vv)
    o = o.reshape(B, dil, L, H, Dh).transpose(0, 2, 1, 3, 4).reshape(B, Sp, H, Dh)[:, :S]
    lse = lse.transpose(0, 1, 2, 4, 3).reshape(B, dil, L, H).transpose(0, 2, 1, 3).reshape(B, Sp, H)[:, :S]
    return o, lse


def dilated_window_decode(q, k_ext, v_ext, bias_tab, dil, n_keys):
    T = q.shape[1]
    Wb = k_ext.shape[1] - T
    j = jnp.arange(n_keys + 1)
    idx = Wb + jnp.arange(T)[:, None] - j[None, :] * dil
    valid = idx >= 0
    idx = jnp.maximum(idx, 0)
    kg = k_ext[:, idx]
    vg = v_ext[:, idx]
    bias = bias_tab[t5_bucket(j * dil)].T
    s = jnp.einsum('bthd,btjhd->bhtj', q, kg) * (HEAD_DIM ** -0.5) + bias[None, :, None, :]
    s = jnp.where(valid[None, None], s, NEG)
    lse = jax.nn.logsumexp(s, axis=-1)
    p = jnp.exp(s - lse[..., None])
    o = jnp.einsum('bhtj,btjhd->bthd', p, vg)
    return o, lse.transpose(0, 2, 1)


def merge_groups(outs, lses):
    o = jnp.stack(outs)
    w = jax.nn.softmax(jnp.stack(lses), axis=0)
    return jnp.einsum('gbsh,gbshd->bshd', w, o)


def attn_layer_prompt(x, norm_g, w_in, q_g, k_g, w_out, rel_bias):
    B, S, _ = x.shape
    q, k, v, gate = attn_project(x, norm_g, w_in, q_g, k_g)
    qf, kf, vf = q.astype(jnp.float32), k.astype(jnp.float32), v.astype(jnp.float32)
    outs, lses, bufs = [], [], []
    for g in range(N_GROUPS):
        hs = slice(g * HEADS_PER_GROUP, (g + 1) * HEADS_PER_GROUP)
        o, l = dilated_band_prompt(qf[:, :, hs], kf[:, :, hs], vf[:, :, hs], rel_bias[:, hs],
                                   DILATIONS[g], WINDOWS[g] // DILATIONS[g])
        outs.append(o)
        lses.append(l)
        keep = min(WINDOWS[g], S)
        bufs.append(jnp.stack([k[:, S - keep:, hs], v[:, S - keep:, hs]], axis=2))
    merged = merge_groups(outs, lses).astype(x.dtype).reshape(B, S, ATTN_WIDTH)
    y = x + (jax.nn.silu(gate) * merged) @ w_out
    return y, bufs


def attn_layer_sample(x, caches, norm_g, w_in, q_g, k_g, w_out, rel_bias):
    B, T, _ = x.shape
    q, k, v, gate = attn_project(x, norm_g, w_in, q_g, k_g)
    qf = q.astype(jnp.float32)
    outs, lses, bufs = [], [], []
    for g in range(N_GROUPS):
        hs = slice(g * HEADS_PER_GROUP, (g + 1) * HEADS_PER_GROUP)
        cache = caches[g]
        Wb = cache.shape[1]
        kv_new = jnp.stack([k[:, :, hs], v[:, :, hs]], axis=2).astype(cache.dtype)
        ext = jnp.concatenate([cache, kv_new], axis=1)
        extf = ext.astype(jnp.float32)
        o, l = dilated_window_decode(qf[:, :, hs], extf[:, :, 0], extf[:, :, 1], rel_bias[:, hs],
                                     DILATIONS[g], WINDOWS[g] // DILATIONS[g])
        outs.append(o)
        lses.append(l)
        bufs.append(ext[:, T:T + Wb])
    merged = merge_groups(outs, lses).astype(x.dtype).reshape(B, T, ATTN_WIDTH)
    y = x + (jax.nn.silu(gate) * merged) @ w_out
    return y, bufs


def causal_dwconv(xpad, w, b):
    out = lax.conv_general_dilated(xpad, w[:, None, :], window_strides=(1,), padding='VALID',
                                   dimension_numbers=('NWC', 'WIO', 'NWC'),
                                   feature_group_count=D_RNN)
    return out + b


def rglru(xc, h0, ga_w, ga_b, gx_w, gx_b, lam):
    B, L, _ = xc.shape
    xf = xc.astype(jnp.float32)
    xb = xf.reshape(B, L, RNN_BLOCKS, RNN_BLOCK_W)
    r = jax.nn.sigmoid(jnp.einsum('blhi,hij->blhj', xb, ga_w.astype(jnp.float32))
                       + ga_b.astype(jnp.float32)).reshape(B, L, D_RNN)
    i = jax.nn.sigmoid(jnp.einsum('blhi,hij->blhj', xb, gx_w.astype(jnp.float32))
                       + gx_b.astype(jnp.float32)).reshape(B, L, D_RNN)
    log_a = -LRU_C * r * jax.nn.softplus(-lam.astype(jnp.float32))
    a = jnp.exp(log_a)
    bterm = jnp.sqrt(-jnp.expm1(2.0 * log_a)) * (i * xf)
    bterm = bterm.at[:, 0].add(a[:, 0] * h0.astype(jnp.float32))

    def comb(lhs, rhs):
        a1, b1 = lhs
        a2, b2 = rhs
        return a1 * a2, a2 * b1 + b2

    _, h = lax.associative_scan(comb, (a, bterm), axis=1)
    return h


def rnn_layer(x, conv_state, h0, norm_g, w_in, conv_w, conv_b, ga_w, ga_b, gx_w, gx_b, lam, w_out):
    hn = rms_norm(x, norm_g)
    xb, gate = jnp.split(hn @ w_in, [D_RNN], axis=-1)
    xcat = jnp.concatenate([conv_state.astype(xb.dtype), xb], axis=1)
    xc = causal_dwconv(xcat, conv_w, conv_b)
    h = rglru(xc, h0, ga_w, ga_b, gx_w, gx_b, lam)
    y = x + (jax.nn.silu(gate) * h.astype(x.dtype)) @ w_out
    return y, h[:, -1].astype(x.dtype), xcat[:, -(CONV_W - 1):]


def setup_inputs(seed: int = 0) -> dict:
    key = jax.random.key(seed)
    ks = iter(jax.random.split(key, 40))

    def nrm(shape, scale):
        return jax.random.normal(next(ks), shape, jnp.float32) * scale

    NA, NR = N_ATTN_LAYERS, N_RNN_LAYERS
    in_w = 3 * QKV_WIDTH + ATTN_WIDTH
    u = jax.random.uniform(next(ks), (NR, D_RNN), jnp.float32, 0.9, 0.999)
    a_base = u ** (1.0 / LRU_C)
    lam = jnp.log(a_base) - jnp.log1p(-a_base)
    kv_shape = lambda w: (NA, DEC_BATCH, min(w, PAST_LEN), 2, HEADS_PER_GROUP, HEAD_DIM)
    return {
        'x_prompt': nrm((BATCH, SEQ, D_MODEL), 1.0),
        'x_sample': nrm((DEC_BATCH, DEC_SEQ, D_MODEL), 1.0),
        'cache_kv_w128': nrm(kv_shape(WINDOWS[0]), 1.0),
        'cache_kv_w512': nrm(kv_shape(WINDOWS[1]), 1.0),
        'cache_kv_w2048': nrm(kv_shape(WINDOWS[2]), 1.0),
        'state_rglru_h': nrm((NR, DEC_BATCH, D_RNN), 0.5),
        'state_rglru_conv': nrm((NR, DEC_BATCH, CONV_W - 1, D_RNN), 1.0),
        'attn_norm': 1.0 + nrm((NA, D_MODEL), 0.05),
        'attn_w_in': nrm((NA, D_MODEL, in_w), D_MODEL ** -0.5),
        'attn_q_norm': 1.0 + nrm((NA, HEAD_DIM), 0.05),
        'attn_k_norm': 1.0 + nrm((NA, HEAD_DIM), 0.05),
        'attn_w_out': nrm((NA, ATTN_WIDTH, D_MODEL), ATTN_WIDTH ** -0.5),
        'rel_bias': nrm((N_BUCKETS, N_HEADS), 0.5),
        'rnn_norm': 1.0 + nrm((NR, D_MODEL), 0.05),
        'rnn_w_in': nrm((NR, D_MODEL, 2 * D_RNN), D_MODEL ** -0.5),
        'rnn_conv_w': nrm((NR, CONV_W, D_RNN), CONV_W ** -0.5),
        'rnn_conv_b': nrm((NR, D_RNN), 0.02),
        'rnn_gate_a_w': nrm((NR, RNN_BLOCKS, RNN_BLOCK_W, RNN_BLOCK_W), RNN_BLOCK_W ** -0.5),
        'rnn_gate_a_b': nrm((NR, RNN_BLOCKS, RNN_BLOCK_W), 0.1),
        'rnn_gate_x_w': nrm((NR, RNN_BLOCKS, RNN_BLOCK_W, RNN_BLOCK_W), RNN_BLOCK_W ** -0.5),
        'rnn_gate_x_b': nrm((NR, RNN_BLOCKS, RNN_BLOCK_W), 0.1),
        'rnn_lambda': lam,
        'rnn_w_out': nrm((NR, D_RNN, D_MODEL), D_RNN ** -0.5),
    }


def reference(x_prompt, x_sample, cache_kv_w128, cache_kv_w512, cache_kv_w2048, state_rglru_h,
              state_rglru_conv, attn_norm, attn_w_in, attn_q_norm, attn_k_norm, attn_w_out, rel_bias,
              rnn_norm, rnn_w_in, rnn_conv_w, rnn_conv_b, rnn_gate_a_w, rnn_gate_a_b, rnn_gate_x_w,
              rnn_gate_x_b, rnn_lambda, rnn_w_out):
    cache_groups = (cache_kv_w128, cache_kv_w512, cache_kv_w2048)
    yp, ys = x_prompt, x_sample
    kv_p = [[] for _ in range(N_GROUPS)]
    kv_s = [[] for _ in range(N_GROUPS)]
    h_p, h_s, c_p, c_s = [], [], [], []
    for i in range(DEPTH):
        li = i // N_MIXERS
        if i % N_MIXERS == 0:
            aw = (attn_norm[li], attn_w_in[li], attn_q_norm[li], attn_k_norm[li], attn_w_out[li], rel_bias)
            yp, bp = attn_layer_prompt(yp, *aw)
            ys, bs = attn_layer_sample(ys, [c[li] for c in cache_groups], *aw)
            for g in range(N_GROUPS):
                kv_p[g].append(bp[g])
                kv_s[g].append(bs[g])
        else:
            rw = (rnn_norm[li], rnn_w_in[li], rnn_conv_w[li], rnn_conv_b[li], rnn_gate_a_w[li],
                  rnn_gate_a_b[li], rnn_gate_x_w[li], rnn_gate_x_b[li], rnn_lambda[li], rnn_w_out[li])
            zc = jnp.zeros((yp.shape[0], CONV_W - 1, D_RNN), yp.dtype)
            zh = jnp.zeros((yp.shape[0], D_RNN), jnp.float32)
            yp, hp, cp = rnn_layer(yp, zc, zh, *rw)
            ys, hs_, cs = rnn_layer(ys, state_rglru_conv[li], state_rglru_h[li], *rw)
            h_p.append(hp)
            h_s.append(hs_)
            c_p.append(cp)
            c_s.append(cs)
    return (yp, ys,
            jnp.stack(kv_p[0]), jnp.stack(kv_s[0]),
            jnp.stack(kv_p[1]), jnp.stack(kv_s[1]),
            jnp.stack(kv_p[2]), jnp.stack(kv_s[2]),
            jnp.stack(h_p), jnp.stack(h_s),
            jnp.stack(c_p), jnp.stack(c_s))
```

```python
import functools
import math

import jax
import jax.numpy as jnp
from jax import lax
from jax.experimental import pallas as pl
from jax.experimental.pallas import tpu as pltpu

F32 = jnp.float32
BF16 = jnp.bfloat16

D_MODEL = 1024
N_GROUPS = 3
WINDOWS = (128, 512, 2048)
DILATIONS = (1, 4, 16)
HEADS_PER_GROUP = 8
HEAD_DIM = 128
N_HEADS = N_GROUPS * HEADS_PER_GROUP
QKV_WIDTH = N_HEADS * HEAD_DIM
ATTN_WIDTH = HEADS_PER_GROUP * HEAD_DIM
Q_BLK = 128
N_KEYS = Q_BLK
SPAN = Q_BLK * DILATIONS[-1]
N_BUCKETS = 32
MAX_DISTANCE = 2048
D_RNN = 1280
RNN_BLOCKS = 10
RNN_BLOCK_W = D_RNN // RNN_BLOCKS
CONV_W = 4
LRU_C = 8.0
EPS = 1e-6
NEG = -1e30

SUBLANES = 8
VMEM_LIMIT = 52 * 1024 * 1024


def _params(semantics):
    return pltpu.CompilerParams(dimension_semantics=semantics, vmem_limit_bytes=VMEM_LIMIT)


def _proj_kernel(x_ref, g_ref, w_ref, *rest, n_norm_tiles):
    if n_norm_tiles:
        hn_ref, o_ref, h_scr = rest
    else:
        o_ref, h_scr = rest
    n = pl.program_id(1)

    @pl.when(n == 0)
    def _():
        x = x_ref[...]
        ms = jnp.mean(x * x, axis=-1, keepdims=True)
        h_scr[...] = (x * lax.rsqrt(ms + EPS) * g_ref[...]).astype(BF16)

    acc = jnp.dot(h_scr[...], w_ref[...], preferred_element_type=F32)
    if not n_norm_tiles:
        o_ref[...] = acc
        return

    @pl.when(n < n_norm_tiles)
    def _():
        for c in range(acc.shape[1] // HEAD_DIM):
            cols = slice(c * HEAD_DIM, (c + 1) * HEAD_DIM)
            a = acc[:, cols]
            ms = jnp.mean(a * a, axis=-1, keepdims=True)
            o_ref[:, cols] = a * lax.rsqrt(ms + EPS) * hn_ref[:, cols]

    @pl.when(n >= n_norm_tiles)
    def _():
        o_ref[...] = acc


def _norm_proj(x, norm_g, w, head_scale, n_norm_cols, tm, tn):
    m_rows, k_dim = x.shape
    n_cols = w.shape[1]
    assert m_rows % tm == 0 and n_cols % tn == 0 and n_norm_cols % tn == 0
    in_specs = [
        pl.BlockSpec((tm, k_dim), lambda m, n: (m, 0)),
        pl.BlockSpec((1, k_dim), lambda m, n: (0, 0)),
        pl.BlockSpec((k_dim, tn), lambda m, n: (0, n)),
    ]
    args = [x, norm_g.reshape(1, k_dim), w]
    if n_norm_cols:
        last_norm_tile = n_norm_cols // tn - 1
        in_specs.append(pl.BlockSpec((1, tn), lambda m, n: (0, jnp.minimum(n, last_norm_tile))))
        args.append(head_scale)
    return pl.pallas_call(
        functools.partial(_proj_kernel, n_norm_tiles=n_norm_cols // tn),
        out_shape=jax.ShapeDtypeStruct((m_rows, n_cols), F32),
        grid=(m_rows // tm, n_cols // tn),
        in_specs=in_specs,
        out_specs=pl.BlockSpec((tm, tn), lambda m, n: (m, n)),
        scratch_shapes=[pltpu.VMEM((tm, k_dim), BF16)],
        compiler_params=_params(("parallel", "arbitrary")),
        name="norm_proj",
    )(*args)


def _out_kernel(u_ref, w_ref, x_ref, y_ref):
    y_ref[...] = x_ref[...] + jnp.dot(u_ref[...].astype(BF16), w_ref[...], preferred_element_type=F32)


def _out_proj(u, w, x, tm):
    m_rows, k_dim = u.shape
    n_cols = w.shape[1]
    assert m_rows % tm == 0
    return pl.pallas_call(
        _out_kernel,
        out_shape=jax.ShapeDtypeStruct((m_rows, n_cols), F32),
        grid=(m_rows // tm,),
        in_specs=[
            pl.BlockSpec((tm, k_dim), lambda m: (m, 0)),
            pl.BlockSpec((k_dim, n_cols), lambda m: (0, 0)),
            pl.BlockSpec((tm, n_cols), lambda m: (m, 0)),
        ],
        out_specs=pl.BlockSpec((tm, n_cols), lambda m: (m, 0)),
        compiler_params=_params(("parallel",)),
        name="out_proj",
    )(u, w, x)


def _attn_prompt_kernel(q0, q1, q2, k0, k1, k2, v0, v1, v2, gate_ref, bias_ref, u_ref,
                        kx, vx, o_scr, l_scr):
    span = pl.program_id(2)
    q_refs = (q0, q1, q2)
    for g, (k_ref, v_ref) in enumerate(((k0, v0), (k1, v1), (k2, v2))):
        tail = slice(SPAN - WINDOWS[g], SPAN)
        tail_cur = slice(2 * SPAN - WINDOWS[g], 2 * SPAN)

        @pl.when(span == 0)
        def _():
            kx[g, tail] = jnp.zeros((WINDOWS[g], HEAD_DIM), F32)
            vx[g, tail] = jnp.zeros((WINDOWS[g], HEAD_DIM), F32)

        @pl.when(span > 0)
        def _():
            kx[g, tail] = kx[g, tail_cur]
            vx[g, tail] = vx[g, tail_cur]

        kx[g, SPAN:2 * SPAN] = k_ref[...]
        vx[g, SPAN:2 * SPAN] = v_ref[...]

    def rows(start, size, stride):
        return pl.ds(start, size) if stride == 1 else pl.ds(start, size, stride=stride)

    def block(g, qs, no_prev):
        d = DILATIONS[g]
        q = q_refs[g][rows(qs, Q_BLK, d), :]
        ks = SPAN - Q_BLK * d + qs
        k = kx[g, rows(ks, 2 * Q_BLK, d), :]
        v = vx[g, rows(ks, 2 * Q_BLK, d), :]
        s = lax.dot_general(q.astype(BF16), k.astype(BF16), (((1,), (1,)), ((), ())),
                            preferred_element_type=F32)
        s = s + bias_ref[g]
        col = lax.broadcasted_iota(jnp.int32, s.shape, 1)
        s = jnp.where(jnp.logical_and(no_prev, col < Q_BLK), NEG, s)
        m = jnp.max(s, axis=-1, keepdims=True)
        p = jnp.exp(s - m)
        l = jnp.sum(p, axis=-1, keepdims=True)
        o = jnp.dot(p.astype(BF16), v.astype(BF16), preferred_element_type=F32) / l
        o_scr[g, rows(qs, Q_BLK, d), :] = o
        l_scr[g, rows(qs, Q_BLK, d), :] = jnp.broadcast_to(m + jnp.log(l), (Q_BLK, HEAD_DIM))

    first_span = span == 0
    for g in range(N_GROUPS):
        d = DILATIONS[g]
        n_blocks = SPAN // (Q_BLK * d)
        for r in range(d):
            def body(n, carry, g=g, d=d, r=r):
                block(g, n * (Q_BLK * d) + r, jnp.logical_and(first_span, n == 0))
                return carry
            lax.fori_loop(0, n_blocks, body, 0)

    def merge(c, carry):
        sl = pl.ds(pl.multiple_of(c * Q_BLK, Q_BLK), Q_BLK)
        l0, l1, l2 = l_scr[0, sl, :], l_scr[1, sl, :], l_scr[2, sl, :]
        mx = jnp.maximum(jnp.maximum(l0, l1), l2)
        e0, e1, e2 = jnp.exp(l0 - mx), jnp.exp(l1 - mx), jnp.exp(l2 - mx)
        merged = (e0 * o_scr[0, sl, :] + e1 * o_scr[1, sl, :] + e2 * o_scr[2, sl, :]) / (e0 + e1 + e2)
        gate = gate_ref[sl, :]
        u_ref[sl, :] = gate * jax.nn.sigmoid(gate) * merged
        return carry
    lax.fori_loop(0, SPAN // Q_BLK, merge, 0)


def _attn_prompt(proj, bias_band, batch, seq):
    n_span = seq // SPAN
    assert seq % SPAN == 0

    def col_spec(first_col_block):
        return pl.BlockSpec((SPAN, HEAD_DIM),
                            lambda b, h, s, c=first_col_block: (b * n_span + s, c + h))

    head_blocks = QKV_WIDTH // HEAD_DIM
    in_specs = (
        [col_spec(g * HEADS_PER_GROUP) for g in range(N_GROUPS)]
        + [col_spec(head_blocks + g * HEADS_PER_GROUP) for g in range(N_GROUPS)]
        + [col_spec(2 * head_blocks + g * HEADS_PER_GROUP) for g in range(N_GROUPS)]
        + [col_spec(3 * head_blocks)]
        + [pl.BlockSpec((None, N_GROUPS, Q_BLK, 2 * Q_BLK), lambda b, h, s: (h, 0, 0, 0))]
    )
    return pl.pallas_call(
        _attn_prompt_kernel,
        out_shape=jax.ShapeDtypeStruct((batch * seq, ATTN_WIDTH), F32),
        grid=(batch, HEADS_PER_GROUP, n_span),
        in_specs=in_specs,
        out_specs=pl.BlockSpec((SPAN, HEAD_DIM), lambda b, h, s: (b * n_span + s, h)),
        scratch_shapes=[
            pltpu.VMEM((N_GROUPS, 2 * SPAN, HEAD_DIM), F32),
            pltpu.VMEM((N_GROUPS, 2 * SPAN, HEAD_DIM), F32),
            pltpu.VMEM((N_GROUPS, SPAN, HEAD_DIM), F32),
            pltpu.VMEM((N_GROUPS, SPAN, HEAD_DIM), F32),
        ],
        compiler_params=_params(("parallel", "parallel", "arbitrary")),
        name="attn_prompt",
    )(*([proj] * 10), bias_band)


def _decode_kernel(q_ref, kn_ref, vn_ref, gate_ref, bias_ref, c0_ref, c1_ref, c2_ref,
                   u_ref, o0_ref, o1_ref, o2_ref, s_scr):
    cache_refs = (c0_ref, c1_ref, c2_ref)
    n_t = q_ref.shape[0]
    outs, lses = [], []
    for g in range(N_GROUPS):
        d = DILATIONS[g]
        heads = slice(g * HEADS_PER_GROUP, (g + 1) * HEADS_PER_GROUP)
        c_ref = cache_refs[g]
        n_cache = N_KEYS - (n_t - 1) // d
        q_t = [q_ref[t, heads, :] for t in range(n_t)]

        def key_tile(kv, t, i, g=g, d=d, c_ref=c_ref, heads=heads):
            row = t + i * d
            if row < WINDOWS[g]:
                return c_ref[row // d, row % d, kv]
            return (kn_ref, vn_ref)[kv][row - WINDOWS[g], heads, :]

        def score(t, k_tile, bias_tile):
            s = jnp.sum(q_t[t] * k_tile, axis=-1, keepdims=True) + bias_tile
            return jnp.broadcast_to(s, (HEADS_PER_GROUP, HEAD_DIM))

        def scores_body(i, carry, g=g, d=d, c_ref=c_ref):
            bias_tile = bias_ref[g, i]
            for t in range(n_t):
                s_scr[t, i] = score(t, c_ref[i + t // d, t % d, 0], bias_tile)
            return carry
        lax.fori_loop(0, n_cache, scores_body, 0)
        for i in range(n_cache, N_KEYS + 1):
            for t in range(n_t):
                s_scr[t, i] = score(t, key_tile(0, t, i), bias_ref[g, i])

        m_t, l_t = [], []
        for t in range(n_t):
            sv = s_scr[t, 0:N_KEYS + 1]
            m = jnp.max(sv, axis=0)
            p = jnp.exp(sv - m)
            s_scr[t, 0:N_KEYS + 1] = p
            m_t.append(m)
            l_t.append(jnp.sum(p, axis=0))

        def pv_body(i, accs, d=d, c_ref=c_ref):
            return tuple(accs[t] + s_scr[t, i] * c_ref[i + t // d, t % d, 1] for t in range(n_t))
        accs = lax.fori_loop(0, n_cache, pv_body,
                             tuple(jnp.zeros((HEADS_PER_GROUP, HEAD_DIM), F32) for _ in range(n_t)))
        accs = list(accs)
        for i in range(n_cache, N_KEYS + 1):
            for t in range(n_t):
                accs[t] = accs[t] + s_scr[t, i] * key_tile(1, t, i)
        outs.append([accs[t] / l_t[t] for t in range(n_t)])
        lses.append([m_t[t] + jnp.log(l_t[t]) for t in range(n_t)])

    for t in range(n_t):
        l0, l1, l2 = lses[0][t], lses[1][t], lses[2][t]
        mx = jnp.maximum(jnp.maximum(l0, l1), l2)
        e0, e1, e2 = jnp.exp(l0 - mx), jnp.exp(l1 - mx), jnp.exp(l2 - mx)
        merged = (e0 * outs[0][t] + e1 * outs[1][t] + e2 * outs[2][t]) / (e0 + e1 + e2)
        gate = gate_ref[t]
        u_ref[t] = gate * jax.nn.sigmoid(gate) * merged

    for g, o_ref in enumerate((o0_ref, o1_ref, o2_ref)):
        heads = slice(g * HEADS_PER_GROUP, (g + 1) * HEADS_PER_GROUP)
        o_ref[:, 0] = kn_ref[:, heads, :]
        o_ref[:, 1] = vn_ref[:, heads, :]


def _decode_attn(layer, q_s, k_s, v_s, gate_s, bias_dec, caches):
    bd, n_t = q_s.shape[:2]
    assert n_t == SUBLANES
    views, view_specs = [], []
    for g, cache in enumerate(caches):
        d = DILATIONS[g]
        n_res = min(d, n_t)
        views.append(cache.reshape(cache.shape[0], bd, N_KEYS, d, 2, HEADS_PER_GROUP, HEAD_DIM))
        view_specs.append(pl.BlockSpec((None, None, N_KEYS, n_res, 2, HEADS_PER_GROUP, HEAD_DIM),
                                       lambda b: (layer, b, 0, 0, 0, 0, 0)))
    tok_spec = pl.BlockSpec((None, n_t, N_HEADS, HEAD_DIM), lambda b: (b, 0, 0, 0))
    u_spec = pl.BlockSpec((None, n_t, HEADS_PER_GROUP, HEAD_DIM), lambda b: (b, 0, 0, 0))
    new_rows_shape = (bd, n_t, 2, HEADS_PER_GROUP, HEAD_DIM)
    new_rows_spec = pl.BlockSpec((None,) + new_rows_shape[1:], lambda b: (b, 0, 0, 0, 0))
    res = pl.pallas_call(
        _decode_kernel,
        out_shape=[jax.ShapeDtypeStruct(gate_s.shape, F32)]
        + [jax.ShapeDtypeStruct(new_rows_shape, c.dtype) for c in caches],
        grid=(bd,),
        in_specs=[tok_spec, tok_spec, tok_spec, u_spec,
                  pl.BlockSpec(bias_dec.shape, lambda b: (0, 0, 0, 0))] + view_specs,
        out_specs=[u_spec] + [new_rows_spec] * N_GROUPS,
        scratch_shapes=[pltpu.VMEM((n_t, N_KEYS + SUBLANES, HEADS_PER_GROUP, HEAD_DIM), F32)],
        compiler_params=_params(("parallel",)),
        name="decode_attn",
    )(q_s, k_s, v_s, gate_s, bias_dec, *views)
    return res[0], tuple(res[1:])


def _cache_update_kernel(c0, c1, c2, n0, n1, n2, o0, o1, o2, sems, *, batch):
    i = pl.program_id(0)
    layer, b = i // batch, i % batch
    copies = []
    for g, (c, n, o) in enumerate(((c0, n0, o0), (c1, n1, o1), (c2, n2, o2))):
        n_new = n.shape[2]
        keep = WINDOWS[g] - n_new
        copies.append(pltpu.make_async_copy(c.at[layer, b, pl.ds(n_new, keep)],
                                            o.at[layer, b, pl.ds(0, keep)], sems.at[g, 0]))
        copies.append(pltpu.make_async_copy(n.at[layer, b], o.at[layer, b, pl.ds(keep, n_new)],
                                            sems.at[g, 1]))
    for cp in copies:
        cp.start()
    for cp in copies:
        cp.wait()


def _cache_update(caches, new_rows):
    n_layers, batch = caches[0].shape[:2]
    any_spec = pl.BlockSpec(memory_space=pl.ANY)
    return pl.pallas_call(
        functools.partial(_cache_update_kernel, batch=batch),
        out_shape=[jax.ShapeDtypeStruct(c.shape, c.dtype) for c in caches],
        grid=(n_layers * batch,),
        in_specs=[any_spec] * (2 * N_GROUPS),
        out_specs=[any_spec] * N_GROUPS,
        scratch_shapes=[pltpu.SemaphoreType.DMA((N_GROUPS, 2))],
        compiler_params=_params(("arbitrary",)),
        name="cache_update",
    )(*caches, *new_rows)


def _scan_rows(a, b, h_prev):
    n_tiles = a.shape[0] // SUBLANES
    a3 = a.reshape(n_tiles, SUBLANES, a.shape[1])
    b3 = b.reshape(n_tiles, SUBLANES, b.shape[1])
    row = lax.broadcasted_iota(jnp.int32, a3.shape, 1)
    shift = 1
    while shift < SUBLANES:
        a_sh = pltpu.roll(a3, shift, axis=1)
        b_sh = pltpu.roll(b3, shift, axis=1)
        keep = row >= shift
        b3 = jnp.where(keep, a3 * b_sh + b3, b3)
        a3 = jnp.where(keep, a3 * a_sh, a3)
        shift *= 2
    tiles = []
    carry = h_prev
    for k in range(n_tiles):
        h_k = b3[k] + a3[k] * carry
        carry = h_k[SUBLANES - 1:SUBLANES]
        tiles.append(h_k)
    return jnp.concatenate(tiles, axis=0), carry


def _rnn_kernel(xb_ref, gate_ref, x_ref, conv0_ref, h0_ref, cw_ref, cb_ref, gaw_ref, gab_ref,
                gxw_ref, gxb_ref, lam_ref, wout_ref, y_ref, hlast_ref, clast_ref,
                xe_scr, h_scr, u_scr):
    c = pl.program_id(1)
    tl = xb_ref.shape[0]
    pad = SUBLANES

    @pl.when(c == 0)
    def _():
        xe_scr[pad - (CONV_W - 1):pad] = conv0_ref[...]
        h_scr[...] = h0_ref[...]

    @pl.when(c > 0)
    def _():
        xe_scr[0:pad] = xe_scr[tl:tl + pad]

    xe_scr[pad:pad + tl] = xb_ref[...]
    clast_ref[...] = xe_scr[pad + tl - (CONV_W - 1):pad + tl]

    for j in range(RNN_BLOCKS):
        cols = slice(j * RNN_BLOCK_W, (j + 1) * RNN_BLOCK_W)
        xc = cb_ref[:, cols]
        for k in range(CONV_W):
            start = pad - (CONV_W - 1) + k
            xc = xc + cw_ref[k:k + 1, cols] * xe_scr[start:start + tl, cols]
        xcb = xc.astype(BF16)
        r = jax.nn.sigmoid(jnp.dot(xcb, gaw_ref[j], preferred_element_type=F32) + gab_ref[:, cols])
        ig = jax.nn.sigmoid(jnp.dot(xcb, gxw_ref[j], preferred_element_type=F32) + gxb_ref[:, cols])
        neg_lam = -lam_ref[:, cols]
        softplus = jnp.maximum(neg_lam, 0.0) + jnp.log1p(jnp.exp(-jnp.abs(neg_lam)))
        log_a = -LRU_C * r * softplus
        a = jnp.exp(log_a)
        bterm = jnp.sqrt(-jnp.tanh(log_a) * (a * a + 1.0)) * (ig * xc)
        h, h_end = _scan_rows(a, bterm, h_scr[:, cols])
        h_scr[:, cols] = h_end
        gate = gate_ref[:, cols]
        u_scr[:, cols] = gate * jax.nn.sigmoid(gate) * h

    hlast_ref[...] = h_scr[...]
    y_ref[...] = x_ref[...] + jnp.dot(u_scr[...].astype(BF16), wout_ref[...], preferred_element_type=F32)


def _rnn_core(proj, x, conv0, h0, conv_w, conv_b, ga_w, ga_b, gx_w, gx_b, lam, w_out, batch, seq, tl):
    n_chunks = seq // tl
    assert seq % tl == 0 and tl % SUBLANES == 0
    row = lambda b, c: (b * n_chunks + c, 0)
    vec = lambda a: a.reshape(1, D_RNN)
    full = lambda shape: pl.BlockSpec(shape, lambda b, c: (0,) * len(shape))
    per_batch = lambda rows_: pl.BlockSpec((None, rows_, D_RNN), lambda b, c: (b, 0, 0))
    y, h_last, c_last = pl.pallas_call(
        _rnn_kernel,
        out_shape=[jax.ShapeDtypeStruct((batch * seq, D_MODEL), F32),
                   jax.ShapeDtypeStruct((batch, 1, D_RNN), F32),
                   jax.ShapeDtypeStruct((batch, CONV_W - 1, D_RNN), F32)],
        grid=(batch, n_chunks),
        in_specs=[
            pl.BlockSpec((tl, D_RNN), row),
            pl.BlockSpec((tl, D_RNN), lambda b, c: (b * n_chunks + c, 1)),
            pl.BlockSpec((tl, D_MODEL), row),
            per_batch(CONV_W - 1),
            per_batch(1),
            full((CONV_W, D_RNN)),
            full((1, D_RNN)),
            full((RNN_BLOCKS, RNN_BLOCK_W, RNN_BLOCK_W)),
            full((1, D_RNN)),
            full((RNN_BLOCKS, RNN_BLOCK_W, RNN_BLOCK_W)),
            full((1, D_RNN)),
            full((1, D_RNN)),
            full((D_RNN, D_MODEL)),
        ],
        out_specs=[pl.BlockSpec((tl, D_MODEL), row), per_batch(1), per_batch(CONV_W - 1)],
        scratch_shapes=[
            pltpu.VMEM((tl + SUBLANES, D_RNN), F32),
            pltpu.VMEM((1, D_RNN), F32),
            pltpu.VMEM((tl, D_RNN), F32),
        ],
        compiler_params=_params(("parallel", "arbitrary")),
        name="rnn_core",
    )(proj, proj, x, conv0, h0.reshape(batch, 1, D_RNN), conv_w, vec(conv_b), ga_w, vec(ga_b),
      gx_w, vec(gx_b), vec(lam), w_out)
    return y, h_last.reshape(batch, D_RNN), c_last


def _t5_bucket(dist):
    n = jnp.maximum(dist, 0)
    max_exact = N_BUCKETS // 2
    nf = jnp.maximum(n, 1).astype(F32)
    large = max_exact + (jnp.log(nf / max_exact) / math.log(MAX_DISTANCE / max_exact)
                         * (N_BUCKETS - max_exact)).astype(jnp.int32)
    large = jnp.minimum(large, N_BUCKETS - 1)
    return jnp.where(n < max_exact, n, large)


def _bias_tables(rel_bias):
    steps = jnp.arange(N_KEYS + 1)
    a = jnp.arange(Q_BLK)[:, None]
    bcol = jnp.arange(2 * Q_BLK)[None, :]
    step = a + Q_BLK - bcol
    in_band = (step >= 0) & (step <= N_KEYS)
    bands, decs = [], []
    for g in range(N_GROUPS):
        heads = slice(g * HEADS_PER_GROUP, (g + 1) * HEADS_PER_GROUP)
        per_step = rel_bias[_t5_bucket(steps * DILATIONS[g])][:, heads]
        band = per_step[jnp.clip(step, 0, N_KEYS)]
        bands.append(jnp.where(in_band[..., None], band, NEG).transpose(2, 0, 1))
        decs.append(jnp.broadcast_to(per_step[::-1][:, :, None],
                                     (N_KEYS + 1, HEADS_PER_GROUP, HEAD_DIM)))
    return jnp.stack(bands, axis=1).astype(F32), jnp.stack(decs).astype(F32)


def kernel(x_prompt, x_sample, cache_kv_w128, cache_kv_w512, cache_kv_w2048, state_rglru_h,
           state_rglru_conv, attn_norm, attn_w_in, attn_q_norm, attn_k_norm, attn_w_out, rel_bias,
           rnn_norm, rnn_w_in, rnn_conv_w, rnn_conv_b, rnn_gate_a_w, rnn_gate_a_b, rnn_gate_x_w,
           rnn_gate_x_b, rnn_lambda, rnn_w_out):
    batch, seq, _ = x_prompt.shape
    bd, n_t, _ = x_sample.shape
    depth = attn_norm.shape[0] + rnn_norm.shape[0]
    caches = (cache_kv_w128, cache_kv_w512, cache_kv_w2048)

    yp = x_prompt.reshape(batch * seq, D_MODEL)
    ys = x_sample.reshape(bd * n_t, D_MODEL)
    tm_p, tm_s = 1024, bd * n_t

    bias_band, bias_dec = _bias_tables(rel_bias)
    kv_p = [[] for _ in range(N_GROUPS)]
    kv_new = [[] for _ in range(N_GROUPS)]
    h_p, h_s, c_p, c_s = [], [], [], []
    for i in range(depth):
        li = i // 2
        if i % 2 == 0:
            w_in = attn_w_in[li].astype(BF16)
            w_out = attn_w_out[li].astype(BF16)
            head_scale = jnp.concatenate([
                jnp.tile(attn_q_norm[li] * (HEAD_DIM ** -0.5), N_HEADS),
                jnp.tile(attn_k_norm[li], N_HEADS)]).reshape(1, 2 * QKV_WIDTH)
            proj_p = _norm_proj(yp, attn_norm[li], w_in, head_scale, 2 * QKV_WIDTH, tm_p, 1024)
            proj_s = _norm_proj(ys, attn_norm[li], w_in, head_scale, 2 * QKV_WIDTH, tm_s, 1024)

            u_p = _attn_prompt(proj_p, bias_band, batch, seq)
            yp = _out_proj(u_p, w_out, yp, tm_p)

            tok = lambda a, heads: a.reshape(bd, n_t, heads, HEAD_DIM)
            q_s = tok(proj_s[:, :QKV_WIDTH], N_HEADS)
            k_s = tok(proj_s[:, QKV_WIDTH:2 * QKV_WIDTH], N_HEADS)
            v_s = tok(proj_s[:, 2 * QKV_WIDTH:3 * QKV_WIDTH], N_HEADS)
            gate_s = tok(proj_s[:, 3 * QKV_WIDTH:], HEADS_PER_GROUP)
            u_s, new_rows = _decode_attn(li, q_s, k_s, v_s, gate_s, bias_dec, caches)
            for g in range(N_GROUPS):
                kv_new[g].append(new_rows[g])
            ys = _out_proj(u_s.reshape(bd * n_t, ATTN_WIDTH), w_out, ys, tm_s)

            for g in range(N_GROUPS):
                keep = min(WINDOWS[g], seq)
                rows_kept = [proj_p[b * seq + seq - keep:(b + 1) * seq] for b in range(batch)]
                k_cols = slice(QKV_WIDTH + g * ATTN_WIDTH, QKV_WIDTH + (g + 1) * ATTN_WIDTH)
                v_cols = slice(2 * QKV_WIDTH + g * ATTN_WIDTH, 2 * QKV_WIDTH + (g + 1) * ATTN_WIDTH)
                kv = jnp.stack([
                    jnp.stack([r[:, k_cols].reshape(keep, HEADS_PER_GROUP, HEAD_DIM),
                               r[:, v_cols].reshape(keep, HEADS_PER_GROUP, HEAD_DIM)], axis=1)
                    for r in rows_kept])
                kv_p[g].append(kv)
        else:
            w_in = rnn_w_in[li].astype(BF16)
            rnn_args = (rnn_conv_w[li], rnn_conv_b[li], rnn_gate_a_w[li].astype(BF16), rnn_gate_a_b[li].reshape(-1),
                        rnn_gate_x_w[li].astype(BF16), rnn_gate_x_b[li].reshape(-1), rnn_lambda[li],
                        rnn_w_out[li].astype(BF16))
            proj_p = _norm_proj(yp, rnn_norm[li], w_in, None, 0, tm_p, D_RNN)
            proj_s = _norm_proj(ys, rnn_norm[li], w_in, None, 0, tm_s, D_RNN)
            yp, hp, cp = _rnn_core(proj_p, yp, jnp.zeros((batch, CONV_W - 1, D_RNN), F32),
                                   jnp.zeros((batch, D_RNN), F32), *rnn_args, batch, seq, 256)
            ys, hs, cs = _rnn_core(proj_s, ys, state_rglru_conv[li], state_rglru_h[li], *rnn_args,
                                   bd, n_t, n_t)
            h_p.append(hp)
            h_s.append(hs)
            c_p.append(cp)
            c_s.append(cs)

    shifted = _cache_update(caches, [jnp.stack(rows) for rows in kv_new])
    return (yp.reshape(batch, seq, D_MODEL), ys.reshape(bd, n_t, D_MODEL),
            jnp.stack(kv_p[0]), shifted[0],
            jnp.stack(kv_p[1]), shifted[1],
            jnp.stack(kv_p[2]), shifted[2],
            jnp.stack(h_p), jnp.stack(h_s),
            jnp.stack(c_p), jnp.stack(c_s))
```

```python
import functools
import math

import jax
import jax.numpy as jnp
from jax import lax
from jax.experimental import pallas as pl
from jax.experimental.pallas import tpu as pltpu

F32 = jnp.float32
BF16 = jnp.bfloat16

D_MODEL = 1024
N_GROUPS = 3
WINDOWS = (128, 512, 2048)
DILATIONS = (1, 4, 16)
HEADS_PER_GROUP = 8
HEAD_DIM = 128
N_HEADS = N_GROUPS * HEADS_PER_GROUP
QKV_WIDTH = N_HEADS * HEAD_DIM
ATTN_WIDTH = HEADS_PER_GROUP * HEAD_DIM
Q_BLK = 128
N_KEYS = Q_BLK
SPAN = Q_BLK * DILATIONS[-1]
PAR_BLOCKS = 4
CACHE_CHUNK_ROWS = 512
SCORE_UNROLL = 8
N_BUCKETS = 32
MAX_DISTANCE = 2048
D_RNN = 1280
RNN_BLOCKS = 10
RNN_BLOCK_W = D_RNN // RNN_BLOCKS
CONV_W = 4
LRU_C = 8.0
EPS = 1e-6
NEG = -1e30

SUBLANES = 8
VMEM_LIMIT = 52 * 1024 * 1024


def _params(semantics):
    return pltpu.CompilerParams(dimension_semantics=semantics, vmem_limit_bytes=VMEM_LIMIT)


def _proj_kernel(x_ref, g_ref, w_ref, *rest, n_norm_tiles):
    if n_norm_tiles:
        hn_ref, o_ref, h_scr = rest
    else:
        o_ref, h_scr = rest
    n = pl.program_id(1)

    @pl.when(n == 0)
    def _():
        x = x_ref[...]
        ms = jnp.mean(x * x, axis=-1, keepdims=True)
        h_scr[...] = (x * lax.rsqrt(ms + EPS) * g_ref[...]).astype(BF16)

    acc = jnp.dot(h_scr[...], w_ref[...], preferred_element_type=F32)
    if not n_norm_tiles:
        o_ref[...] = acc
        return

    @pl.when(n < n_norm_tiles)
    def _():
        for c in range(acc.shape[1] // HEAD_DIM):
            cols = slice(c * HEAD_DIM, (c + 1) * HEAD_DIM)
            a = acc[:, cols]
            ms = jnp.mean(a * a, axis=-1, keepdims=True)
            o_ref[:, cols] = a * lax.rsqrt(ms + EPS) * hn_ref[:, cols]

    @pl.when(n >= n_norm_tiles)
    def _():
        o_ref[...] = acc


def _norm_proj(x, norm_g, w, head_scale, n_norm_cols, tm, tn):
    m_rows, k_dim = x.shape
    n_cols = w.shape[1]
    assert m_rows % tm == 0 and n_cols % tn == 0 and n_norm_cols % tn == 0
    in_specs = [
        pl.BlockSpec((tm, k_dim), lambda m, n: (m, 0)),
        pl.BlockSpec((1, k_dim), lambda m, n: (0, 0)),
        pl.BlockSpec((k_dim, tn), lambda m, n: (0, n)),
    ]
    args = [x, norm_g.reshape(1, k_dim), w]
    if n_norm_cols:
        last_norm_tile = n_norm_cols // tn - 1
        in_specs.append(pl.BlockSpec((1, tn), lambda m, n: (0, jnp.minimum(n, last_norm_tile))))
        args.append(head_scale)
    return pl.pallas_call(
        functools.partial(_proj_kernel, n_norm_tiles=n_norm_cols // tn),
        out_shape=jax.ShapeDtypeStruct((m_rows, n_cols), F32),
        grid=(m_rows // tm, n_cols // tn),
        in_specs=in_specs,
        out_specs=pl.BlockSpec((tm, tn), lambda m, n: (m, n)),
        scratch_shapes=[pltpu.VMEM((tm, k_dim), BF16)],
        compiler_params=_params(("parallel", "arbitrary")),
        name="norm_proj",
    )(*args)


def _out_kernel(u_ref, w_ref, x_ref, y_ref):
    y_ref[...] = x_ref[...] + jnp.dot(u_ref[...].astype(BF16), w_ref[...], preferred_element_type=F32)


def _out_proj(u, w, x, tm):
    m_rows, k_dim = u.shape
    n_cols = w.shape[1]
    assert m_rows % tm == 0
    return pl.pallas_call(
        _out_kernel,
        out_shape=jax.ShapeDtypeStruct((m_rows, n_cols), F32),
        grid=(m_rows // tm,),
        in_specs=[
            pl.BlockSpec((tm, k_dim), lambda m: (m, 0)),
            pl.BlockSpec((k_dim, n_cols), lambda m: (0, 0)),
            pl.BlockSpec((tm, n_cols), lambda m: (m, 0)),
        ],
        out_specs=pl.BlockSpec((tm, n_cols), lambda m: (m, 0)),
        compiler_params=_params(("parallel",)),
        name="out_proj",
    )(u, w, x)


def _attn_prompt_kernel(q0, q1, q2, k0, k1, k2, v0, v1, v2, gate_ref, bias_ref, u_ref,
                        kx, vx, o_scr, l_scr):
    span = pl.program_id(2)
    q_refs = (q0, q1, q2)
    for g, (k_ref, v_ref) in enumerate(((k0, v0), (k1, v1), (k2, v2))):
        tail = slice(SPAN - WINDOWS[g], SPAN)
        tail_cur = slice(2 * SPAN - WINDOWS[g], 2 * SPAN)

        @pl.when(span == 0)
        def _():
            kx[g, tail] = jnp.zeros((WINDOWS[g], HEAD_DIM), F32)
            vx[g, tail] = jnp.zeros((WINDOWS[g], HEAD_DIM), F32)

        @pl.when(span > 0)
        def _():
            kx[g, tail] = kx[g, tail_cur]
            vx[g, tail] = vx[g, tail_cur]

        kx[g, SPAN:2 * SPAN] = k_ref[...]
        vx[g, SPAN:2 * SPAN] = v_ref[...]

    def rows(start, size, stride):
        return pl.ds(start, size) if stride == 1 else pl.ds(start, size, stride=stride)

    def blocks(g, starts, no_prevs):
        d = DILATIONS[g]
        each = range(len(starts))
        q = [q_refs[g][rows(qs, Q_BLK, d), :].astype(BF16) for qs in starts]
        k = [kx[g, rows(SPAN - Q_BLK * d + qs, 2 * Q_BLK, d), :].astype(BF16) for qs in starts]
        v = [vx[g, rows(SPAN - Q_BLK * d + qs, 2 * Q_BLK, d), :].astype(BF16) for qs in starts]
        s = [lax.dot_general(q[j], k[j], (((1,), (1,)), ((), ())), preferred_element_type=F32) + bias_ref[g]
             for j in each]
        in_prev = lax.broadcasted_iota(jnp.int32, s[0].shape, 1) < Q_BLK
        s = [s[j] if no_prevs[j] is None else jnp.where(jnp.logical_and(no_prevs[j], in_prev), NEG, s[j])
             for j in each]
        m = [jnp.max(s[j], axis=-1, keepdims=True) for j in each]
        p = [jnp.exp(s[j] - m[j]) for j in each]
        l = [jnp.sum(p[j], axis=-1, keepdims=True) for j in each]
        o = [jnp.dot(p[j].astype(BF16), v[j], preferred_element_type=F32) / l[j] for j in each]
        for j, qs in enumerate(starts):
            o_scr[g, rows(qs, Q_BLK, d), :] = o[j]
            l_scr[g, rows(qs, Q_BLK, d), :] = jnp.broadcast_to(m[j] + jnp.log(l[j]), (Q_BLK, HEAD_DIM))

    first_span = span == 0
    def body0(it, carry):
        base = pl.multiple_of(it * (PAR_BLOCKS * Q_BLK), PAR_BLOCKS * Q_BLK)
        blocks(0, [base + j * Q_BLK for j in range(PAR_BLOCKS)],
               [jnp.logical_and(first_span, it == 0)] + [None] * (PAR_BLOCKS - 1))
        return carry
    lax.fori_loop(0, SPAN // (PAR_BLOCKS * Q_BLK), body0, 0)

    def body1(n, carry):
        first = jnp.logical_and(first_span, n == 0)
        blocks(1, [n * (Q_BLK * DILATIONS[1]) + r for r in range(DILATIONS[1])], [first] * DILATIONS[1])
        return carry
    lax.fori_loop(0, SPAN // (Q_BLK * DILATIONS[1]), body1, 0)

    for r0 in range(0, DILATIONS[2], PAR_BLOCKS):
        blocks(2, [r0 + j for j in range(PAR_BLOCKS)], [first_span] * PAR_BLOCKS)

    def merge(c, carry):
        sl = pl.ds(pl.multiple_of(c * Q_BLK, Q_BLK), Q_BLK)
        l0, l1, l2 = l_scr[0, sl, :], l_scr[1, sl, :], l_scr[2, sl, :]
        mx = jnp.maximum(jnp.maximum(l0, l1), l2)
        e0, e1, e2 = jnp.exp(l0 - mx), jnp.exp(l1 - mx), jnp.exp(l2 - mx)
        merged = (e0 * o_scr[0, sl, :] + e1 * o_scr[1, sl, :] + e2 * o_scr[2, sl, :]) / (e0 + e1 + e2)
        gate = gate_ref[sl, :]
        u_ref[sl, :] = gate * jax.nn.sigmoid(gate) * merged
        return carry
    lax.fori_loop(0, SPAN // Q_BLK, merge, 0)


def _attn_prompt(proj, bias_band, batch, seq):
    n_span = seq // SPAN
    assert seq % SPAN == 0

    def col_spec(first_col_block):
        return pl.BlockSpec((SPAN, HEAD_DIM),
                            lambda b, h, s, c=first_col_block: (b * n_span + s, c + h))

    head_blocks = QKV_WIDTH // HEAD_DIM
    in_specs = (
        [col_spec(g * HEADS_PER_GROUP) for g in range(N_GROUPS)]
        + [col_spec(head_blocks + g * HEADS_PER_GROUP) for g in range(N_GROUPS)]
        + [col_spec(2 * head_blocks + g * HEADS_PER_GROUP) for g in range(N_GROUPS)]
        + [col_spec(3 * head_blocks)]
        + [pl.BlockSpec((None, N_GROUPS, Q_BLK, 2 * Q_BLK), lambda b, h, s: (h, 0, 0, 0))]
    )
    return pl.pallas_call(
        _attn_prompt_kernel,
        out_shape=jax.ShapeDtypeStruct((batch * seq, ATTN_WIDTH), F32),
        grid=(batch, HEADS_PER_GROUP, n_span),
        in_specs=in_specs,
        out_specs=pl.BlockSpec((SPAN, HEAD_DIM), lambda b, h, s: (b * n_span + s, h)),
        scratch_shapes=[
            pltpu.VMEM((N_GROUPS, 2 * SPAN, HEAD_DIM), F32),
            pltpu.VMEM((N_GROUPS, 2 * SPAN, HEAD_DIM), F32),
            pltpu.VMEM((N_GROUPS, SPAN, HEAD_DIM), F32),
            pltpu.VMEM((N_GROUPS, SPAN, HEAD_DIM), F32),
        ],
        compiler_params=_params(("parallel", "parallel", "arbitrary")),
        name="attn_prompt",
    )(*([proj] * 10), bias_band)


def _decode_kernel(q_ref, kn_ref, vn_ref, gate_ref, bias_ref, c0_ref, c1_ref, c2_ref,
                   u_ref, o0_ref, o1_ref, o2_ref, s_scr):
    cache_refs = (c0_ref, c1_ref, c2_ref)
    n_t = q_ref.shape[0]
    outs, lses = [], []
    for g in range(N_GROUPS):
        d = DILATIONS[g]
        heads = slice(g * HEADS_PER_GROUP, (g + 1) * HEADS_PER_GROUP)
        c_ref = cache_refs[g]
        n_cache = N_KEYS - (n_t - 1) // d
        q_t = [q_ref[t, heads, :] for t in range(n_t)]

        def key_tile(kv, t, i, g=g, d=d, c_ref=c_ref, heads=heads):
            row = t + i * d
            if row < WINDOWS[g]:
                return c_ref[row // d, row % d, kv]
            return (kn_ref, vn_ref)[kv][row - WINDOWS[g], heads, :]

        def score(t, k_tile, bias_tile):
            s = jnp.sum(q_t[t] * k_tile, axis=-1, keepdims=True) + bias_tile
            return jnp.broadcast_to(s, (HEADS_PER_GROUP, HEAD_DIM))

        def scores_body(i, carry, g=g, d=d, c_ref=c_ref):
            bias_tile = bias_ref[g, i]
            for t in range(n_t):
                s_scr[t, i] = score(t, c_ref[i + t // d, t % d, 0], bias_tile)
            return carry
        lax.fori_loop(0, n_cache, scores_body, 0, unroll=SCORE_UNROLL)
        for i in range(n_cache, N_KEYS + 1):
            for t in range(n_t):
                s_scr[t, i] = score(t, key_tile(0, t, i), bias_ref[g, i])

        m_t, l_t = [], []
        for t in range(n_t):
            sv = s_scr[t, 0:N_KEYS + 1]
            m = jnp.max(sv, axis=0)
            p = jnp.exp(sv - m)
            s_scr[t, 0:N_KEYS + 1] = p
            m_t.append(m)
            l_t.append(jnp.sum(p, axis=0))

        def pv_body(i, accs, d=d, c_ref=c_ref):
            return tuple(accs[t] + s_scr[t, i] * c_ref[i + t // d, t % d, 1] for t in range(n_t))
        accs = lax.fori_loop(0, n_cache, pv_body,
                             tuple(jnp.zeros((HEADS_PER_GROUP, HEAD_DIM), F32) for _ in range(n_t)))
        accs = list(accs)
        for i in range(n_cache, N_KEYS + 1):
            for t in range(n_t):
                accs[t] = accs[t] + s_scr[t, i] * key_tile(1, t, i)
        outs.append([accs[t] / l_t[t] for t in range(n_t)])
        lses.append([m_t[t] + jnp.log(l_t[t]) for t in range(n_t)])

    for t in range(n_t):
        l0, l1, l2 = lses[0][t], lses[1][t], lses[2][t]
        mx = jnp.maximum(jnp.maximum(l0, l1), l2)
        e0, e1, e2 = jnp.exp(l0 - mx), jnp.exp(l1 - mx), jnp.exp(l2 - mx)
        merged = (e0 * outs[0][t] + e1 * outs[1][t] + e2 * outs[2][t]) / (e0 + e1 + e2)
        gate = gate_ref[t]
        u_ref[t] = gate * jax.nn.sigmoid(gate) * merged

    for g, o_ref in enumerate((o0_ref, o1_ref, o2_ref)):
        heads = slice(g * HEADS_PER_GROUP, (g + 1) * HEADS_PER_GROUP)
        o_ref[:, 0] = kn_ref[:, heads, :]
        o_ref[:, 1] = vn_ref[:, heads, :]


def _decode_attn(layer, q_s, k_s, v_s, gate_s, bias_dec, caches):
    bd, n_t = q_s.shape[:2]
    assert n_t == SUBLANES
    views, view_specs = [], []
    for g, cache in enumerate(caches):
        d = DILATIONS[g]
        n_res = min(d, n_t)
        views.append(cache.reshape(cache.shape[0], bd, N_KEYS, d, 2, HEADS_PER_GROUP, HEAD_DIM))
        view_specs.append(pl.BlockSpec((None, None, N_KEYS, n_res, 2, HEADS_PER_GROUP, HEAD_DIM),
                                       lambda b: (layer, b, 0, 0, 0, 0, 0)))
    tok_spec = pl.BlockSpec((None, n_t, N_HEADS, HEAD_DIM), lambda b: (b, 0, 0, 0))
    u_spec = pl.BlockSpec((None, n_t, HEADS_PER_GROUP, HEAD_DIM), lambda b: (b, 0, 0, 0))
    new_rows_shape = (bd, n_t, 2, HEADS_PER_GROUP, HEAD_DIM)
    new_rows_spec = pl.BlockSpec((None,) + new_rows_shape[1:], lambda b: (b, 0, 0, 0, 0))
    res = pl.pallas_call(
        _decode_kernel,
        out_shape=[jax.ShapeDtypeStruct(gate_s.shape, F32)]
        + [jax.ShapeDtypeStruct(new_rows_shape, c.dtype) for c in caches],
        grid=(bd,),
        in_specs=[tok_spec, tok_spec, tok_spec, u_spec,
                  pl.BlockSpec(bias_dec.shape, lambda b: (0, 0, 0, 0))] + view_specs,
        out_specs=[u_spec] + [new_rows_spec] * N_GROUPS,
        scratch_shapes=[pltpu.VMEM((n_t, N_KEYS + SUBLANES, HEADS_PER_GROUP, HEAD_DIM), F32)],
        compiler_params=_params(("parallel",)),
        name="decode_attn",
    )(q_s, k_s, v_s, gate_s, bias_dec, *views)
    return res[0], tuple(res[1:])


def _cache_update_kernel(cur_ref, next_ref, new_ref, o_ref):
    rows_, n_new = cur_ref.shape[0], new_ref.shape[0]
    o_ref[0:rows_ - n_new] = cur_ref[n_new:rows_]
    last = pl.program_id(1) == pl.num_programs(1) - 1

    @pl.when(jnp.logical_not(last))
    def _():
        o_ref[rows_ - n_new:rows_] = next_ref[...]

    @pl.when(last)
    def _():
        o_ref[rows_ - n_new:rows_] = new_ref[...]


def _cache_update(cache, new_rows):
    n_layers, batch, window = cache.shape[:3]
    n_new = new_rows.shape[2]
    chunk = min(window, CACHE_CHUNK_ROWS)
    assert window % chunk == 0 and chunk % n_new == 0
    tile = cache.shape[3:]
    zeros = (0,) * len(tile)
    next_blocks = chunk // n_new
    last_block = window // n_new - 1
    return pl.pallas_call(
        _cache_update_kernel,
        out_shape=jax.ShapeDtypeStruct(cache.shape, cache.dtype),
        grid=(n_layers * batch, window // chunk),
        in_specs=[
            pl.BlockSpec((None, None, chunk) + tile, lambda i, c: (i // batch, i % batch, c) + zeros),
            pl.BlockSpec((None, None, n_new) + tile,
                         lambda i, c: (i // batch, i % batch, jnp.minimum((c + 1) * next_blocks, last_block))
                         + zeros),
            pl.BlockSpec((None, None, n_new) + tile, lambda i, c: (i // batch, i % batch, 0) + zeros),
        ],
        out_specs=pl.BlockSpec((None, None, chunk) + tile, lambda i, c: (i // batch, i % batch, c) + zeros),
        compiler_params=_params(("parallel", "arbitrary")),
        name="cache_update",
    )(cache, cache, new_rows)


def _scan_rows(a, b, h_prev):
    n_tiles = a.shape[0] // SUBLANES
    a3 = a.reshape(n_tiles, SUBLANES, a.shape[1])
    b3 = b.reshape(n_tiles, SUBLANES, b.shape[1])
    row = lax.broadcasted_iota(jnp.int32, a3.shape, 1)
    shift = 1
    while shift < SUBLANES:
        a_sh = pltpu.roll(a3, shift, axis=1)
        b_sh = pltpu.roll(b3, shift, axis=1)
        keep = row >= shift
        b3 = jnp.where(keep, a3 * b_sh + b3, b3)
        a3 = jnp.where(keep, a3 * a_sh, a3)
        shift *= 2
    tiles = []
    carry = h_prev
    for k in range(n_tiles):
        h_k = b3[k] + a3[k] * carry
        carry = h_k[SUBLANES - 1:SUBLANES]
        tiles.append(h_k)
    return jnp.concatenate(tiles, axis=0), carry


def _rnn_kernel(xb_ref, gate_ref, x_ref, conv0_ref, h0_ref, cw_ref, cb_ref, gaw_ref, gab_ref,
                gxw_ref, gxb_ref, lam_ref, wout_ref, y_ref, hlast_ref, clast_ref,
                xe_scr, h_scr, u_scr):
    c = pl.program_id(1)
    tl = xb_ref.shape[0]
    pad = SUBLANES

    @pl.when(c == 0)
    def _():
        xe_scr[pad - (CONV_W - 1):pad] = conv0_ref[...]
        h_scr[...] = h0_ref[...]

    @pl.when(c > 0)
    def _():
        xe_scr[0:pad] = xe_scr[tl:tl + pad]

    xe_scr[pad:pad + tl] = xb_ref[...]
    clast_ref[...] = xe_scr[pad + tl - (CONV_W - 1):pad + tl]

    for j in range(RNN_BLOCKS):
        cols = slice(j * RNN_BLOCK_W, (j + 1) * RNN_BLOCK_W)
        xc = cb_ref[:, cols]
        for k in range(CONV_W):
            start = pad - (CONV_W - 1) + k
            xc = xc + cw_ref[k:k + 1, cols] * xe_scr[start:start + tl, cols]
        xcb = xc.astype(BF16)
        r = jax.nn.sigmoid(jnp.dot(xcb, gaw_ref[j], preferred_element_type=F32) + gab_ref[:, cols])
        ig = jax.nn.sigmoid(jnp.dot(xcb, gxw_ref[j], preferred_element_type=F32) + gxb_ref[:, cols])
        neg_lam = -lam_ref[:, cols]
        softplus = jnp.maximum(neg_lam, 0.0) + jnp.log1p(jnp.exp(-jnp.abs(neg_lam)))
        log_a = -LRU_C * r * softplus
        a = jnp.exp(log_a)
        bterm = jnp.sqrt(-jnp.tanh(log_a) * (a * a + 1.0)) * (ig * xc)
        h, h_end = _scan_rows(a, bterm, h_scr[:, cols])
        h_scr[:, cols] = h_end
        gate = gate_ref[:, cols]
        u_scr[:, cols] = gate * jax.nn.sigmoid(gate) * h

    hlast_ref[...] = h_scr[...]
    y_ref[...] = x_ref[...] + jnp.dot(u_scr[...].astype(BF16), wout_ref[...], preferred_element_type=F32)


def _rnn_core(proj, x, conv0, h0, conv_w, conv_b, ga_w, ga_b, gx_w, gx_b, lam, w_out, batch, seq, tl):
    n_chunks = seq // tl
    assert seq % tl == 0 and tl % SUBLANES == 0
    row = lambda b, c: (b * n_chunks + c, 0)
    vec = lambda a: a.reshape(1, D_RNN)
    full = lambda shape: pl.BlockSpec(shape, lambda b, c: (0,) * len(shape))
    per_batch = lambda rows_: pl.BlockSpec((None, rows_, D_RNN), lambda b, c: (b, 0, 0))
    y, h_last, c_last = pl.pallas_call(
        _rnn_kernel,
        out_shape=[jax.ShapeDtypeStruct((batch * seq, D_MODEL), F32),
                   jax.ShapeDtypeStruct((batch, 1, D_RNN), F32),
                   jax.ShapeDtypeStruct((batch, CONV_W - 1, D_RNN), F32)],
        grid=(batch, n_chunks),
        in_specs=[
            pl.BlockSpec((tl, D_RNN), row),
            pl.BlockSpec((tl, D_RNN), lambda b, c: (b * n_chunks + c, 1)),
            pl.BlockSpec((tl, D_MODEL), row),
            per_batch(CONV_W - 1),
            per_batch(1),
            full((CONV_W, D_RNN)),
            full((1, D_RNN)),
            full((RNN_BLOCKS, RNN_BLOCK_W, RNN_BLOCK_W)),
            full((1, D_RNN)),
            full((RNN_BLOCKS, RNN_BLOCK_W, RNN_BLOCK_W)),
            full((1, D_RNN)),
            full((1, D_RNN)),
            full((D_RNN, D_MODEL)),
        ],
        out_specs=[pl.BlockSpec((tl, D_MODEL), row), per_batch(1), per_batch(CONV_W - 1)],
        scratch_shapes=[
            pltpu.VMEM((tl + SUBLANES, D_RNN), F32),
            pltpu.VMEM((1, D_RNN), F32),
            pltpu.VMEM((tl, D_RNN), F32),
        ],
        compiler_params=_params(("parallel", "arbitrary")),
        name="rnn_core",
    )(proj, proj, x, conv0, h0.reshape(batch, 1, D_RNN), conv_w, vec(conv_b), ga_w, vec(ga_b),
      gx_w, vec(gx_b), vec(lam), w_out)
    return y, h_last.reshape(batch, D_RNN), c_last


def _t5_bucket(dist):
    n = jnp.maximum(dist, 0)
    max_exact = N_BUCKETS // 2
    nf = jnp.maximum(n, 1).astype(F32)
    large = max_exact + (jnp.log(nf / max_exact) / math.log(MAX_DISTANCE / max_exact)
                         * (N_BUCKETS - max_exact)).astype(jnp.int32)
    large = jnp.minimum(large, N_BUCKETS - 1)
    return jnp.where(n < max_exact, n, large)


def _bias_tables(rel_bias):
    steps = jnp.arange(N_KEYS + 1)
    period = 3 * Q_BLK
    bands, decs = [], []
    for g in range(N_GROUPS):
        heads = slice(g * HEADS_PER_GROUP, (g + 1) * HEADS_PER_GROUP)
        per_step = rel_bias[_t5_bucket(steps * DILATIONS[g])][:, heads]
        row = jnp.concatenate([per_step[::-1].T,
                               jnp.full((HEADS_PER_GROUP, period - (N_KEYS + 1)), NEG, F32)], axis=1)
        band = jnp.tile(row, (1, Q_BLK))[:, :Q_BLK * (period - 1)]
        bands.append(band.reshape(HEADS_PER_GROUP, Q_BLK, period - 1)[:, :, :2 * Q_BLK])
        decs.append(jnp.broadcast_to(per_step[::-1][:, :, None],
                                     (N_KEYS + 1, HEADS_PER_GROUP, HEAD_DIM)))
    return jnp.stack(bands, axis=1).astype(F32), jnp.stack(decs).astype(F32)


def kernel(x_prompt, x_sample, cache_kv_w128, cache_kv_w512, cache_kv_w2048, state_rglru_h,
           state_rglru_conv, attn_norm, attn_w_in, attn_q_norm, attn_k_norm, attn_w_out, rel_bias,
           rnn_norm, rnn_w_in, rnn_conv_w, rnn_conv_b, rnn_gate_a_w, rnn_gate_a_b, rnn_gate_x_w,
           rnn_gate_x_b, rnn_lambda, rnn_w_out):
    batch, seq, _ = x_prompt.shape
    bd, n_t, _ = x_sample.shape
    depth = attn_norm.shape[0] + rnn_norm.shape[0]
    caches = (cache_kv_w128, cache_kv_w512, cache_kv_w2048)

    yp = x_prompt.reshape(batch * seq, D_MODEL)
    ys = x_sample.reshape(bd * n_t, D_MODEL)
    tm_p, tm_s = 1024, bd * n_t

    bias_band, bias_dec = _bias_tables(rel_bias)
    kv_p = [[] for _ in range(N_GROUPS)]
    kv_new = [[] for _ in range(N_GROUPS)]
    h_p, h_s, c_p, c_s = [], [], [], []
    for i in range(depth):
        li = i // 2
        if i % 2 == 0:
            w_in = attn_w_in[li].astype(BF16)
            w_out = attn_w_out[li].astype(BF16)
            head_scale = jnp.concatenate([
                jnp.tile(attn_q_norm[li] * (HEAD_DIM ** -0.5), N_HEADS),
                jnp.tile(attn_k_norm[li], N_HEADS)]).reshape(1, 2 * QKV_WIDTH)
            proj_p = _norm_proj(yp, attn_norm[li], w_in, head_scale, 2 * QKV_WIDTH, tm_p, 1024)
            proj_s = _norm_proj(ys, attn_norm[li], w_in, head_scale, 2 * QKV_WIDTH, tm_s, 1024)

            u_p = _attn_prompt(proj_p, bias_band, batch, seq)
            yp = _out_proj(u_p, w_out, yp, tm_p)

            tok = lambda a, heads: a.reshape(bd, n_t, heads, HEAD_DIM)
            q_s = tok(proj_s[:, :QKV_WIDTH], N_HEADS)
            k_s = tok(proj_s[:, QKV_WIDTH:2 * QKV_WIDTH], N_HEADS)
            v_s = tok(proj_s[:, 2 * QKV_WIDTH:3 * QKV_WIDTH], N_HEADS)
            gate_s = tok(proj_s[:, 3 * QKV_WIDTH:], HEADS_PER_GROUP)
            u_s, new_rows = _decode_attn(li, q_s, k_s, v_s, gate_s, bias_dec, caches)
            for g in range(N_GROUPS):
                kv_new[g].append(new_rows[g])
            ys = _out_proj(u_s.reshape(bd * n_t, ATTN_WIDTH), w_out, ys, tm_s)

            for g in range(N_GROUPS):
                keep = min(WINDOWS[g], seq)
                rows_kept = [proj_p[b * seq + seq - keep:(b + 1) * seq] for b in range(batch)]
                k_cols = slice(QKV_WIDTH + g * ATTN_WIDTH, QKV_WIDTH + (g + 1) * ATTN_WIDTH)
                v_cols = slice(2 * QKV_WIDTH + g * ATTN_WIDTH, 2 * QKV_WIDTH + (g + 1) * ATTN_WIDTH)
                kv = jnp.stack([
                    jnp.stack([r[:, k_cols].reshape(keep, HEADS_PER_GROUP, HEAD_DIM),
                               r[:, v_cols].reshape(keep, HEADS_PER_GROUP, HEAD_DIM)], axis=1)
                    for r in rows_kept])
                kv_p[g].append(kv)
        else:
            w_in = rnn_w_in[li].astype(BF16)
            rnn_args = (rnn_conv_w[li], rnn_conv_b[li], rnn_gate_a_w[li].astype(BF16), rnn_gate_a_b[li].reshape(-1),
                        rnn_gate_x_w[li].astype(BF16), rnn_gate_x_b[li].reshape(-1), rnn_lambda[li],
                        rnn_w_out[li].astype(BF16))
            proj_p = _norm_proj(yp, rnn_norm[li], w_in, None, 0, tm_p, D_RNN)
            proj_s = _norm_proj(ys, rnn_norm[li], w_in, None, 0, tm_s, D_RNN)
            yp, hp, cp = _rnn_core(proj_p, yp, jnp.zeros((batch, CONV_W - 1, D_RNN), F32),
                                   jnp.zeros((batch, D_RNN), F32), *rnn_args, batch, seq, 256)
            ys, hs, cs = _rnn_core(proj_s, ys, state_rglru_conv[li], state_rglru_h[li], *rnn_args,
                                   bd, n_t, n_t)
            h_p.append(hp)
            h_s.append(hs)
            c_p.append(cp)
            c_s.append(cs)

    shifted = [_cache_update(caches[g], jnp.stack(kv_new[g])) for g in range(N_GROUPS)]
    return (yp.reshape(batch, seq, D_MODEL), ys.reshape(bd, n_t, D_MODEL),
            jnp.stack(kv_p[0]), shifted[0],
            jnp.stack(kv_p[1]), shifted[1],
            jnp.stack(kv_p[2]), shifted[2],
            jnp.stack(h_p), jnp.stack(h_s),
            jnp.stack(c_p), jnp.stack(c_s))
```

```python
import functools
import math

import jax
import jax.numpy as jnp
from jax import lax
from jax.experimental import pallas as pl
from jax.experimental.pallas import tpu as pltpu

F32 = jnp.float32
BF16 = jnp.bfloat16

D_MODEL = 1024
N_GROUPS = 3
WINDOWS = (128, 512, 2048)
DILATIONS = (1, 4, 16)
HEADS_PER_GROUP = 8
HEAD_DIM = 128
N_HEADS = N_GROUPS * HEADS_PER_GROUP
QKV_WIDTH = N_HEADS * HEAD_DIM
ATTN_WIDTH = HEADS_PER_GROUP * HEAD_DIM
Q_BLK = 128
N_KEYS = Q_BLK
SPAN = Q_BLK * DILATIONS[-1]
CACHE_CHUNK_ROWS = 512
SCORE_UNROLL = 8
N_BUCKETS = 32
MAX_DISTANCE = 2048
D_RNN = 1280
RNN_BLOCKS = 10
RNN_BLOCK_W = D_RNN // RNN_BLOCKS
CONV_W = 4
LRU_C = 8.0
EPS = 1e-6
NEG = -1e30

SUBLANES = 8
MXU_COLS = 256
VMEM_LIMIT = 52 * 1024 * 1024


def _params(semantics):
    return pltpu.CompilerParams(dimension_semantics=semantics, vmem_limit_bytes=VMEM_LIMIT)


def _residue_rows(start, size, stride):
    return pl.ds(start, size) if stride == 1 else pl.ds(start, size, stride=stride)


def _proj_kernel(*refs, n_norm_tiles, row_orders, tiles_per_order, n_slabs):
    x_refs, (g_ref, w_ref), rest = refs[:n_slabs], refs[n_slabs:n_slabs + 2], refs[n_slabs + 2:]
    if n_norm_tiles:
        hs_ref, o_ref, h_scr = rest
    else:
        o_ref, h_scr = rest
    n = pl.program_id(1)
    tm = x_refs[0].shape[0]
    k_dim = n_slabs * HEAD_DIM

    @pl.when(n == 0)
    def _():
        for i, d in enumerate(row_orders):
            for r in range(d):
                xs = [x_ref[_residue_rows(r, tm // d, d), :] for x_ref in x_refs]
                ms = jnp.sum(sum(x * x for x in xs), axis=-1, keepdims=True) * (1.0 / k_dim)
                inv = lax.rsqrt(ms + EPS)
                for c, x in enumerate(xs):
                    cols = slice(c * HEAD_DIM, (c + 1) * HEAD_DIM)
                    h_scr[i, r * (tm // d):(r + 1) * (tm // d), cols] = (x * inv * g_ref[:, cols]).astype(BF16)

    if len(row_orders) == 1:
        h = h_scr[0]
    else:
        order = jnp.where(n < tiles_per_order * len(row_orders) * 3, (n // tiles_per_order) % len(row_orders), 0)
        h = h_scr[order]
    if not n_norm_tiles:
        o_ref[...] = jnp.dot(h, w_ref[...], preferred_element_type=F32)
        return

    is_norm_tile = n < n_norm_tiles
    for c in range(o_ref.shape[1] // MXU_COLS):
        acc = jnp.dot(h, w_ref[:, c * MXU_COLS:(c + 1) * MXU_COLS], preferred_element_type=F32)
        for hh in range(MXU_COLS // HEAD_DIM):
            cols = slice(c * MXU_COLS + hh * HEAD_DIM, c * MXU_COLS + (hh + 1) * HEAD_DIM)
            a = acc[:, hh * HEAD_DIM:(hh + 1) * HEAD_DIM]
            ms = jnp.mean(a * a, axis=-1, keepdims=True)
            f = jnp.where(is_norm_tile, lax.rsqrt(ms + EPS), 1.0)
            o_ref[:, cols] = a * f * hs_ref[:, cols]


def _norm_proj(x, norm_g, w, head_scale, n_norm_cols, tm, tn, row_orders=(1,)):
    m_rows, k_dim = x.shape
    n_cols = w.shape[1]
    assert m_rows % tm == 0 and n_cols % tn == 0 and n_norm_cols % tn == 0
    assert len(row_orders) == 1 or ATTN_WIDTH % tn == 0
    n_slabs = k_dim // HEAD_DIM
    in_specs = [pl.BlockSpec((tm, HEAD_DIM), lambda m, n, c=c: (m, c)) for c in range(n_slabs)] + [
        pl.BlockSpec((1, k_dim), lambda m, n: (0, 0)),
        pl.BlockSpec((k_dim, tn), lambda m, n: (0, n)),
    ]
    args = [x] * n_slabs + [norm_g.reshape(1, k_dim), w]
    if n_norm_cols:
        in_specs.append(pl.BlockSpec((1, tn), lambda m, n: (0, n)))
        args.append(head_scale)
    return pl.pallas_call(
        functools.partial(_proj_kernel, n_norm_tiles=n_norm_cols // tn, row_orders=row_orders,
                          tiles_per_order=ATTN_WIDTH // tn, n_slabs=n_slabs),
        out_shape=jax.ShapeDtypeStruct((m_rows, n_cols), F32),
        grid=(m_rows // tm, n_cols // tn),
        in_specs=in_specs,
        out_specs=pl.BlockSpec((tm, tn), lambda m, n: (m, n)),
        scratch_shapes=[pltpu.VMEM((len(row_orders), tm, k_dim), BF16)],
        compiler_params=_params(("parallel", "arbitrary")),
        name="norm_proj",
    )(*args)


def _out_kernel(u_ref, w_ref, x_ref, y_ref):
    y_ref[...] = x_ref[...] + jnp.dot(u_ref[...].astype(BF16), w_ref[...], preferred_element_type=F32)


def _out_proj(u, w, x, tm):
    m_rows, k_dim = u.shape
    n_cols = w.shape[1]
    assert m_rows % tm == 0
    return pl.pallas_call(
        _out_kernel,
        out_shape=jax.ShapeDtypeStruct((m_rows, n_cols), F32),
        grid=(m_rows // tm,),
        in_specs=[
            pl.BlockSpec((tm, k_dim), lambda m: (m, 0)),
            pl.BlockSpec((k_dim, n_cols), lambda m: (0, 0)),
            pl.BlockSpec((tm, n_cols), lambda m: (m, 0)),
        ],
        out_specs=pl.BlockSpec((tm, n_cols), lambda m: (m, 0)),
        compiler_params=_params(("parallel",)),
        name="out_proj",
    )(u, w, x)


def _attn_prompt_kernel(q0, q1, q2, k0, k1, k2, v0, v1, v2, gate_ref, bias_ref, u_ref,
                        kx, vx, o_scr, l_scr):
    span = pl.program_id(2)
    q_refs = (q0, q1, q2)
    n_blk = SPAN // Q_BLK
    cur, prev = span % 2, 1 - span % 2

    @pl.when(span == 0)
    def _():
        kx[:, 1] = jnp.zeros((N_GROUPS, SPAN, HEAD_DIM), BF16)
        vx[:, 1] = jnp.zeros((N_GROUPS, SPAN, HEAD_DIM), BF16)

    for g, (k_ref, v_ref) in enumerate(((k0, v0), (k1, v1), (k2, v2))):
        kx[g, cur] = k_ref[...].astype(BF16)
        vx[g, cur] = v_ref[...].astype(BF16)

    first_span = span == 0

    def blk_rows(bi):
        return slice(bi * Q_BLK, (bi + 1) * Q_BLK)

    def prev_block(g, bi):
        per_residue = n_blk // DILATIONS[g]
        if bi % per_residue:
            return cur, bi - 1, None
        return prev, bi + per_residue - 1, first_span

    def token_rows(g, bi):
        d = DILATIONS[g]
        per_residue = n_blk // d
        return _residue_rows((bi % per_residue) * Q_BLK * d + bi // per_residue, Q_BLK, d)

    for g in range(N_GROUPS):
        each = range(n_blk)
        prevs = [prev_block(g, bi) for bi in each]
        q = [q_refs[g][blk_rows(bi), :].astype(BF16) for bi in each]
        k = [jnp.concatenate([kx[g, prevs[bi][0], blk_rows(prevs[bi][1]), :], kx[g, cur, blk_rows(bi), :]], axis=0)
             for bi in each]
        v = [jnp.concatenate([vx[g, prevs[bi][0], blk_rows(prevs[bi][1]), :], vx[g, cur, blk_rows(bi), :]], axis=0)
             for bi in each]
        s = [lax.dot_general(q[bi], k[bi], (((1,), (1,)), ((), ())), preferred_element_type=F32) + bias_ref[g]
             for bi in each]
        in_prev = lax.broadcasted_iota(jnp.int32, s[0].shape, 1) < Q_BLK
        s = [s[bi] if prevs[bi][2] is None else jnp.where(jnp.logical_and(prevs[bi][2], in_prev), NEG, s[bi])
             for bi in each]
        m = [jnp.max(s[bi], axis=-1, keepdims=True) for bi in each]
        p = [jnp.exp(s[bi] - m[bi]) for bi in each]
        l = [jnp.sum(p[bi], axis=-1, keepdims=True) for bi in each]
        o = [jnp.dot(p[bi].astype(BF16), v[bi], preferred_element_type=F32) / l[bi] for bi in each]
        for bi in each:
            o_scr[g, token_rows(g, bi), :] = o[bi]
            l_scr[g, token_rows(g, bi), :] = jnp.broadcast_to(m[bi] + jnp.log(l[bi]), (Q_BLK, HEAD_DIM))

    for c in range(n_blk):
        rows_ = blk_rows(c)
        l0, l1, l2 = l_scr[0, rows_, :], l_scr[1, rows_, :], l_scr[2, rows_, :]
        mx = jnp.maximum(jnp.maximum(l0, l1), l2)
        e0, e1, e2 = jnp.exp(l0 - mx), jnp.exp(l1 - mx), jnp.exp(l2 - mx)
        merged = (e0 * o_scr[0, rows_, :] + e1 * o_scr[1, rows_, :] + e2 * o_scr[2, rows_, :]) / (e0 + e1 + e2)
        gate = gate_ref[rows_, :]
        u_ref[rows_, :] = gate * jax.nn.sigmoid(gate) * merged


def _attn_prompt(proj, bias_band, batch, seq):
    n_span = seq // SPAN
    assert seq % SPAN == 0

    def col_spec(first_col_block):
        return pl.BlockSpec((SPAN, HEAD_DIM),
                            lambda b, h, s, c=first_col_block: (b * n_span + s, c + h))

    head_blocks = QKV_WIDTH // HEAD_DIM
    in_specs = (
        [col_spec(g * HEADS_PER_GROUP) for g in range(N_GROUPS)]
        + [col_spec(head_blocks + g * HEADS_PER_GROUP) for g in range(N_GROUPS)]
        + [col_spec(2 * head_blocks + g * HEADS_PER_GROUP) for g in range(N_GROUPS)]
        + [col_spec(3 * head_blocks)]
        + [pl.BlockSpec((None, N_GROUPS, Q_BLK, 2 * Q_BLK), lambda b, h, s: (h, 0, 0, 0))]
    )
    return pl.pallas_call(
        _attn_prompt_kernel,
        out_shape=jax.ShapeDtypeStruct((batch * seq, ATTN_WIDTH), F32),
        grid=(batch, HEADS_PER_GROUP, n_span),
        in_specs=in_specs,
        out_specs=pl.BlockSpec((SPAN, HEAD_DIM), lambda b, h, s: (b * n_span + s, h)),
        scratch_shapes=[
            pltpu.VMEM((N_GROUPS, 2, SPAN, HEAD_DIM), BF16),
            pltpu.VMEM((N_GROUPS, 2, SPAN, HEAD_DIM), BF16),
            pltpu.VMEM((N_GROUPS, SPAN, HEAD_DIM), F32),
            pltpu.VMEM((N_GROUPS, SPAN, HEAD_DIM), F32),
        ],
        compiler_params=_params(("parallel", "parallel", "arbitrary")),
        name="attn_prompt",
    )(*([proj] * 10), bias_band)


def _decode_kernel(q_ref, kn_ref, vn_ref, gate_ref, bias_ref, c0_ref, c1_ref, c2_ref,
                   u_ref, o0_ref, o1_ref, o2_ref, s_scr):
    cache_refs = (c0_ref, c1_ref, c2_ref)
    n_t = q_ref.shape[0]
    outs, lses = [], []
    for g in range(N_GROUPS):
        d = DILATIONS[g]
        heads = slice(g * HEADS_PER_GROUP, (g + 1) * HEADS_PER_GROUP)
        c_ref = cache_refs[g]
        n_cache = N_KEYS - (n_t - 1) // d
        q_t = [q_ref[t, heads, :] for t in range(n_t)]

        def key_tile(kv, t, i, g=g, d=d, c_ref=c_ref, heads=heads):
            row = t + i * d
            if row < WINDOWS[g]:
                return c_ref[row // d, row % d, kv]
            return (kn_ref, vn_ref)[kv][row - WINDOWS[g], heads, :]

        def score(t, k_tile, bias_tile):
            s = jnp.sum(q_t[t] * k_tile, axis=-1, keepdims=True) + bias_tile
            return jnp.broadcast_to(s, (HEADS_PER_GROUP, HEAD_DIM))

        def scores_body(i, carry, g=g, d=d, c_ref=c_ref):
            bias_tile = bias_ref[g, i]
            for t in range(n_t):
                s_scr[t, i] = score(t, c_ref[i + t // d, t % d, 0], bias_tile)
            return carry
        lax.fori_loop(0, n_cache, scores_body, 0, unroll=SCORE_UNROLL)
        for i in range(n_cache, N_KEYS + 1):
            for t in range(n_t):
                s_scr[t, i] = score(t, key_tile(0, t, i), bias_ref[g, i])

        m_t, l_t = [], []
        for t in range(n_t):
            sv = s_scr[t, 0:N_KEYS + 1]
            m = jnp.max(sv, axis=0)
            p = jnp.exp(sv - m)
            s_scr[t, 0:N_KEYS + 1] = p
            m_t.append(m)
            l_t.append(jnp.sum(p, axis=0))

        def pv_body(i, accs, d=d, c_ref=c_ref):
            return tuple(accs[t] + s_scr[t, i] * c_ref[i + t // d, t % d, 1] for t in range(n_t))
        accs = lax.fori_loop(0, n_cache, pv_body,
                             tuple(jnp.zeros((HEADS_PER_GROUP, HEAD_DIM), F32) for _ in range(n_t)))
        accs = list(accs)
        for i in range(n_cache, N_KEYS + 1):
            for t in range(n_t):
                accs[t] = accs[t] + s_scr[t, i] * key_tile(1, t, i)
        outs.append([accs[t] / l_t[t] for t in range(n_t)])
        lses.append([m_t[t] + jnp.log(l_t[t]) for t in range(n_t)])

    for t in range(n_t):
        l0, l1, l2 = lses[0][t], lses[1][t], lses[2][t]
        mx = jnp.maximum(jnp.maximum(l0, l1), l2)
        e0, e1, e2 = jnp.exp(l0 - mx), jnp.exp(l1 - mx), jnp.exp(l2 - mx)
        merged = (e0 * outs[0][t] + e1 * outs[1][t] + e2 * outs[2][t]) / (e0 + e1 + e2)
        gate = gate_ref[t]
        u_ref[t] = gate * jax.nn.sigmoid(gate) * merged

    for g, o_ref in enumerate((o0_ref, o1_ref, o2_ref)):
        heads = slice(g * HEADS_PER_GROUP, (g + 1) * HEADS_PER_GROUP)
        o_ref[:, 0] = kn_ref[:, heads, :]
        o_ref[:, 1] = vn_ref[:, heads, :]


def _decode_attn(layer, q_s, k_s, v_s, gate_s, bias_dec, caches):
    bd, n_t = q_s.shape[:2]
    assert n_t == SUBLANES
    views, view_specs = [], []
    for g, cache in enumerate(caches):
        d = DILATIONS[g]
        n_res = min(d, n_t)
        views.append(cache.reshape(cache.shape[0], bd, N_KEYS, d, 2, HEADS_PER_GROUP, HEAD_DIM))
        view_specs.append(pl.BlockSpec((None, None, N_KEYS, n_res, 2, HEADS_PER_GROUP, HEAD_DIM),
                                       lambda b: (layer, b, 0, 0, 0, 0, 0)))
    tok_spec = pl.BlockSpec((None, n_t, N_HEADS, HEAD_DIM), lambda b: (b, 0, 0, 0))
    u_spec = pl.BlockSpec((None, n_t, HEADS_PER_GROUP, HEAD_DIM), lambda b: (b, 0, 0, 0))
    new_rows_shape = (bd, n_t, 2, HEADS_PER_GROUP, HEAD_DIM)
    new_rows_spec = pl.BlockSpec((None,) + new_rows_shape[1:], lambda b: (b, 0, 0, 0, 0))
    res = pl.pallas_call(
        _decode_kernel,
        out_shape=[jax.ShapeDtypeStruct(gate_s.shape, F32)]
        + [jax.ShapeDtypeStruct(new_rows_shape, c.dtype) for c in caches],
        grid=(bd,),
        in_specs=[tok_spec, tok_spec, tok_spec, u_spec,
                  pl.BlockSpec(bias_dec.shape, lambda b: (0, 0, 0, 0))] + view_specs,
        out_specs=[u_spec] + [new_rows_spec] * N_GROUPS,
        scratch_shapes=[pltpu.VMEM((n_t, N_KEYS + SUBLANES, HEADS_PER_GROUP, HEAD_DIM), F32)],
        compiler_params=_params(("parallel",)),
        name="decode_attn",
    )(q_s, k_s, v_s, gate_s, bias_dec, *views)
    return res[0], tuple(res[1:])


def _cache_update_kernel(cur_ref, next_ref, new_ref, o_ref):
    rows_, n_new = cur_ref.shape[0], new_ref.shape[0]
    o_ref[0:rows_ - n_new] = cur_ref[n_new:rows_]
    last = pl.program_id(1) == pl.num_programs(1) - 1

    @pl.when(jnp.logical_not(last))
    def _():
        o_ref[rows_ - n_new:rows_] = next_ref[...]

    @pl.when(last)
    def _():
        o_ref[rows_ - n_new:rows_] = new_ref[...]


def _cache_update(cache, new_rows):
    n_layers, batch, window = cache.shape[:3]
    n_new = new_rows.shape[2]
    chunk = min(window, CACHE_CHUNK_ROWS)
    assert window % chunk == 0 and chunk % n_new == 0
    tile = cache.shape[3:]
    zeros = (0,) * len(tile)
    next_blocks = chunk // n_new
    last_block = window // n_new - 1
    return pl.pallas_call(
        _cache_update_kernel,
        out_shape=jax.ShapeDtypeStruct(cache.shape, cache.dtype),
        grid=(n_layers * batch, window // chunk),
        in_specs=[
            pl.BlockSpec((None, None, chunk) + tile, lambda i, c: (i // batch, i % batch, c) + zeros),
            pl.BlockSpec((None, None, n_new) + tile,
                         lambda i, c: (i // batch, i % batch, jnp.minimum((c + 1) * next_blocks, last_block))
                         + zeros),
            pl.BlockSpec((None, None, n_new) + tile, lambda i, c: (i // batch, i % batch, 0) + zeros),
        ],
        out_specs=pl.BlockSpec((None, None, chunk) + tile, lambda i, c: (i // batch, i % batch, c) + zeros),
        compiler_params=_params(("parallel", "arbitrary")),
        name="cache_update",
    )(cache, cache, new_rows)


def _scan_rows(a, b, h_prev):
    n_tiles = a.shape[0] // SUBLANES
    a3 = a.reshape(n_tiles, SUBLANES, a.shape[1])
    b3 = b.reshape(n_tiles, SUBLANES, b.shape[1])
    row = lax.broadcasted_iota(jnp.int32, a3.shape, 1)
    shift = 1
    while shift < SUBLANES:
        a_sh = pltpu.roll(a3, shift, axis=1)
        b_sh = pltpu.roll(b3, shift, axis=1)
        keep = row >= shift
        b3 = jnp.where(keep, a3 * b_sh + b3, b3)
        a3 = jnp.where(keep, a3 * a_sh, a3)
        shift *= 2
    tiles = []
    carry = h_prev
    for k in range(n_tiles):
        h_k = b3[k] + a3[k] * carry
        carry = h_k[SUBLANES - 1:SUBLANES]
        tiles.append(h_k)
    return jnp.concatenate(tiles, axis=0), carry


def _rnn_kernel(xb_ref, gate_ref, x_ref, conv0_ref, h0_ref, cw_ref, cb_ref, gaw_ref, gab_ref,
                gxw_ref, gxb_ref, lam_ref, wout_ref, y_ref, hlast_ref, clast_ref,
                xe_scr, h_scr, u_scr):
    c = pl.program_id(1)
    tl = xb_ref.shape[0]
    pad = SUBLANES

    @pl.when(c == 0)
    def _():
        xe_scr[pad - (CONV_W - 1):pad] = conv0_ref[...]
        h_scr[...] = h0_ref[...]

    @pl.when(c > 0)
    def _():
        xe_scr[0:pad] = xe_scr[tl:tl + pad]

    xe_scr[pad:pad + tl] = xb_ref[...]
    clast_ref[...] = xe_scr[pad + tl - (CONV_W - 1):pad + tl]

    for j in range(RNN_BLOCKS):
        cols = slice(j * RNN_BLOCK_W, (j + 1) * RNN_BLOCK_W)
        xc = cb_ref[:, cols]
        for k in range(CONV_W):
            start = pad - (CONV_W - 1) + k
            xc = xc + cw_ref[k:k + 1, cols] * xe_scr[start:start + tl, cols]
        xcb = xc.astype(BF16)
        r = jax.nn.sigmoid(jnp.dot(xcb, gaw_ref[j], preferred_element_type=F32) + gab_ref[:, cols])
        ig = jax.nn.sigmoid(jnp.dot(xcb, gxw_ref[j], preferred_element_type=F32) + gxb_ref[:, cols])
        neg_lam = -lam_ref[:, cols]
        softplus = jnp.maximum(neg_lam, 0.0) + jnp.log1p(jnp.exp(-jnp.abs(neg_lam)))
        log_a = -LRU_C * r * softplus
        a = jnp.exp(log_a)
        bterm = jnp.sqrt(-jnp.tanh(log_a) * (a * a + 1.0)) * (ig * xc)
        h, h_end = _scan_rows(a, bterm, h_scr[:, cols])
        h_scr[:, cols] = h_end
        gate = gate_ref[:, cols]
        u_scr[:, cols] = gate * jax.nn.sigmoid(gate) * h

    hlast_ref[...] = h_scr[...]
    y_ref[...] = x_ref[...] + jnp.dot(u_scr[...].astype(BF16), wout_ref[...], preferred_element_type=F32)


def _rnn_core(proj, x, conv0, h0, conv_w, conv_b, ga_w, ga_b, gx_w, gx_b, lam, w_out, batch, seq, tl):
    n_chunks = seq // tl
    assert seq % tl == 0 and tl % SUBLANES == 0
    row = lambda b, c: (b * n_chunks + c, 0)
    vec = lambda a: a.reshape(1, D_RNN)
    full = lambda shape: pl.BlockSpec(shape, lambda b, c: (0,) * len(shape))
    per_batch = lambda rows_: pl.BlockSpec((None, rows_, D_RNN), lambda b, c: (b, 0, 0))
    y, h_last, c_last = pl.pallas_call(
        _rnn_kernel,
        out_shape=[jax.ShapeDtypeStruct((batch * seq, D_MODEL), F32),
                   jax.ShapeDtypeStruct((batch, 1, D_RNN), F32),
                   jax.ShapeDtypeStruct((batch, CONV_W - 1, D_RNN), F32)],
        grid=(batch, n_chunks),
        in_specs=[
            pl.BlockSpec((tl, D_RNN), row),
            pl.BlockSpec((tl, D_RNN), lambda b, c: (b * n_chunks + c, 1)),
            pl.BlockSpec((tl, D_MODEL), row),
            per_batch(CONV_W - 1),
            per_batch(1),
            full((CONV_W, D_RNN)),
            full((1, D_RNN)),
            full((RNN_BLOCKS, RNN_BLOCK_W, RNN_BLOCK_W)),
            full((1, D_RNN)),
            full((RNN_BLOCKS, RNN_BLOCK_W, RNN_BLOCK_W)),
            full((1, D_RNN)),
            full((1, D_RNN)),
            full((D_RNN, D_MODEL)),
        ],
        out_specs=[pl.BlockSpec((tl, D_MODEL), row), per_batch(1), per_batch(CONV_W - 1)],
        scratch_shapes=[
            pltpu.VMEM((tl + SUBLANES, D_RNN), F32),
            pltpu.VMEM((1, D_RNN), F32),
            pltpu.VMEM((tl, D_RNN), F32),
        ],
        compiler_params=_params(("parallel", "arbitrary")),
        name="rnn_core",
    )(proj, proj, x, conv0, h0.reshape(batch, 1, D_RNN), conv_w, vec(conv_b), ga_w, vec(ga_b),
      gx_w, vec(gx_b), vec(lam), w_out)
    return y, h_last.reshape(batch, D_RNN), c_last


def _t5_bucket(dist):
    n = jnp.maximum(dist, 0)
    max_exact = N_BUCKETS // 2
    nf = jnp.maximum(n, 1).astype(F32)
    large = max_exact + (jnp.log(nf / max_exact) / math.log(MAX_DISTANCE / max_exact)
                         * (N_BUCKETS - max_exact)).astype(jnp.int32)
    large = jnp.minimum(large, N_BUCKETS - 1)
    return jnp.where(n < max_exact, n, large)


def _bias_tables(rel_bias):
    steps = jnp.arange(N_KEYS + 1)
    period = 3 * Q_BLK
    bands, decs = [], []
    for g in range(N_GROUPS):
        heads = slice(g * HEADS_PER_GROUP, (g + 1) * HEADS_PER_GROUP)
        per_step = rel_bias[_t5_bucket(steps * DILATIONS[g])][:, heads]
        row = jnp.concatenate([per_step[::-1].T,
                               jnp.full((HEADS_PER_GROUP, period - (N_KEYS + 1)), NEG, F32)], axis=1)
        band = jnp.tile(row, (1, Q_BLK))[:, :Q_BLK * (period - 1)]
        bands.append(band.reshape(HEADS_PER_GROUP, Q_BLK, period - 1)[:, :, :2 * Q_BLK])
        decs.append(jnp.broadcast_to(per_step[::-1][:, :, None],
                                     (N_KEYS + 1, HEADS_PER_GROUP, HEAD_DIM)))
    return jnp.stack(bands, axis=1).astype(F32), jnp.stack(decs).astype(F32)


def kernel(x_prompt, x_sample, cache_kv_w128, cache_kv_w512, cache_kv_w2048, state_rglru_h,
           state_rglru_conv, attn_norm, attn_w_in, attn_q_norm, attn_k_norm, attn_w_out, rel_bias,
           rnn_norm, rnn_w_in, rnn_conv_w, rnn_conv_b, rnn_gate_a_w, rnn_gate_a_b, rnn_gate_x_w,
           rnn_gate_x_b, rnn_lambda, rnn_w_out):
    batch, seq, _ = x_prompt.shape
    bd, n_t, _ = x_sample.shape
    depth = attn_norm.shape[0] + rnn_norm.shape[0]
    caches = (cache_kv_w128, cache_kv_w512, cache_kv_w2048)

    yp = x_prompt.reshape(batch * seq, D_MODEL)
    ys = x_sample.reshape(bd * n_t, D_MODEL)
    tm_p, tm_s = 1024, bd * n_t

    bias_band, bias_dec = _bias_tables(rel_bias)
    kv_p = [[] for _ in range(N_GROUPS)]
    kv_new = [[] for _ in range(N_GROUPS)]
    h_p, h_s, c_p, c_s = [], [], [], []
    for i in range(depth):
        li = i // 2
        if i % 2 == 0:
            w_in = attn_w_in[li].astype(BF16)
            w_out = attn_w_out[li].astype(BF16)
            head_scale = jnp.concatenate([
                jnp.tile(attn_q_norm[li] * (HEAD_DIM ** -0.5), N_HEADS),
                jnp.tile(attn_k_norm[li], N_HEADS),
                jnp.ones((QKV_WIDTH + ATTN_WIDTH,), F32)]).reshape(1, 3 * QKV_WIDTH + ATTN_WIDTH)
            proj_p = _norm_proj(yp, attn_norm[li], w_in, head_scale, 2 * QKV_WIDTH, SPAN, 512,
                                row_orders=DILATIONS)
            proj_s = _norm_proj(ys, attn_norm[li], w_in, head_scale, 2 * QKV_WIDTH, tm_s, 1024)

            u_p = _attn_prompt(proj_p, bias_band, batch, seq)
            yp = _out_proj(u_p, w_out, yp, tm_p)

            tok = lambda a, heads: a.reshape(bd, n_t, heads, HEAD_DIM)
            q_s = tok(proj_s[:, :QKV_WIDTH], N_HEADS)
            k_s = tok(proj_s[:, QKV_WIDTH:2 * QKV_WIDTH], N_HEADS)
            v_s = tok(proj_s[:, 2 * QKV_WIDTH:3 * QKV_WIDTH], N_HEADS)
            gate_s = tok(proj_s[:, 3 * QKV_WIDTH:], HEADS_PER_GROUP)
            u_s, new_rows = _decode_attn(li, q_s, k_s, v_s, gate_s, bias_dec, caches)
            for g in range(N_GROUPS):
                kv_new[g].append(new_rows[g])
            ys = _out_proj(u_s.reshape(bd * n_t, ATTN_WIDTH), w_out, ys, tm_s)

            for g in range(N_GROUPS):
                d = DILATIONS[g]
                keep = min(WINDOWS[g], seq)
                assert keep <= SPAN and keep % d == 0
                k_cols = slice(QKV_WIDTH + g * ATTN_WIDTH, QKV_WIDTH + (g + 1) * ATTN_WIDTH)
                v_cols = slice(2 * QKV_WIDTH + g * ATTN_WIDTH, 2 * QKV_WIDTH + (g + 1) * ATTN_WIDTH)

                def newest(b, cols, d=d, keep=keep):
                    last_span = proj_p[(b + 1) * seq - SPAN:(b + 1) * seq, cols]
                    by_residue = last_span.reshape(d, SPAN // d, HEADS_PER_GROUP, HEAD_DIM)
                    kept = by_residue[:, (SPAN - keep) // d:]
                    return kept.transpose(1, 0, 2, 3).reshape(keep, HEADS_PER_GROUP, HEAD_DIM)

                kv_p[g].append(jnp.stack([jnp.stack([newest(b, k_cols), newest(b, v_cols)], axis=1)
                                          for b in range(batch)]))
        else:
            w_in = rnn_w_in[li].astype(BF16)
            rnn_args = (rnn_conv_w[li], rnn_conv_b[li], rnn_gate_a_w[li].astype(BF16), rnn_gate_a_b[li].reshape(-1),
                        rnn_gate_x_w[li].astype(BF16), rnn_gate_x_b[li].reshape(-1), rnn_lambda[li],
                        rnn_w_out[li].astype(BF16))
            proj_p = _norm_proj(yp, rnn_norm[li], w_in, None, 0, tm_p, D_RNN)
            proj_s = _norm_proj(ys, rnn_norm[li], w_in, None, 0, tm_s, D_RNN)
            yp, hp, cp = _rnn_core(proj_p, yp, jnp.zeros((batch, CONV_W - 1, D_RNN), F32),
                                   jnp.zeros((batch, D_RNN), F32), *rnn_args, batch, seq, 256)
            ys, hs, cs = _rnn_core(proj_s, ys, state_rglru_conv[li], state_rglru_h[li], *rnn_args,
                                   bd, n_t, n_t)
            h_p.append(hp)
            h_s.append(hs)
            c_p.append(cp)
            c_s.append(cs)

    shifted = [_cache_update(caches[g], jnp.stack(kv_new[g])) for g in range(N_GROUPS)]
    return (yp.reshape(batch, seq, D_MODEL), ys.reshape(bd, n_t, D_MODEL),
            jnp.stack(kv_p[0]), shifted[0],
            jnp.stack(kv_p[1]), shifted[1],
            jnp.stack(kv_p[2]), shifted[2],
            jnp.stack(h_p), jnp.stack(h_s),
            jnp.stack(c_p), jnp.stack(c_s))
```

```python
import functools
import math

import jax
import jax.numpy as jnp
from jax import lax
from jax.experimental import pallas as pl
from jax.experimental.pallas import tpu as pltpu

F32 = jnp.float32
BF16 = jnp.bfloat16

D_MODEL = 1024
N_GROUPS = 3
WINDOWS = (128, 512, 2048)
DILATIONS = (1, 4, 16)
HEADS_PER_GROUP = 8
HEAD_DIM = 128
N_HEADS = N_GROUPS * HEADS_PER_GROUP
QKV_WIDTH = N_HEADS * HEAD_DIM
ATTN_WIDTH = HEADS_PER_GROUP * HEAD_DIM
Q_BLK = 128
N_KEYS = Q_BLK
SPAN = Q_BLK * DILATIONS[-1]
RNN_CHUNK = 256
CACHE_CHUNK_ROWS = 512
SCORE_KEYS = 8
SCORE_UNROLL = 4
N_BUCKETS = 32
MAX_DISTANCE = 2048
D_RNN = 1280
RNN_BLOCKS = 10
RNN_BLOCK_W = D_RNN // RNN_BLOCKS
CONV_W = 4
LRU_C = 8.0
EPS = 1e-6
NEG = -1e30

SUBLANES = 8
MXU_COLS = 256
VMEM_LIMIT = 52 * 1024 * 1024


def _params(semantics):
    return pltpu.CompilerParams(dimension_semantics=semantics, vmem_limit_bytes=VMEM_LIMIT)


def _residue_rows(start, size, stride):
    return pl.ds(start, size) if stride == 1 else pl.ds(start, size, stride=stride)


def _proj_kernel(*refs, n_norm_tiles, row_orders, tiles_per_order, n_slabs):
    x_refs, (g_ref, w_ref), rest = refs[:n_slabs], refs[n_slabs:n_slabs + 2], refs[n_slabs + 2:]
    if n_norm_tiles:
        hs_ref, o_ref, h_scr = rest
    else:
        o_ref, h_scr = rest
    n = pl.program_id(1)
    tm = x_refs[0].shape[0]
    k_dim = n_slabs * HEAD_DIM

    @pl.when(n == 0)
    def _():
        for i, d in enumerate(row_orders):
            for r in range(d):
                xs = [x_ref[_residue_rows(r, tm // d, d), :] for x_ref in x_refs]
                ms = jnp.sum(sum(x * x for x in xs), axis=-1, keepdims=True) * (1.0 / k_dim)
                inv = lax.rsqrt(ms + EPS)
                for c, x in enumerate(xs):
                    cols = slice(c * HEAD_DIM, (c + 1) * HEAD_DIM)
                    h_scr[i, r * (tm // d):(r + 1) * (tm // d), cols] = (x * inv * g_ref[:, cols]).astype(BF16)

    if len(row_orders) == 1:
        h = h_scr[0]
    else:
        order = jnp.where(n < tiles_per_order * len(row_orders) * 3, (n // tiles_per_order) % len(row_orders), 0)
        h = h_scr[order]
    if not n_norm_tiles:
        o_ref[...] = jnp.dot(h, w_ref[...], preferred_element_type=F32)
        return

    is_norm_tile = n < n_norm_tiles
    for c in range(o_ref.shape[1] // MXU_COLS):
        acc = jnp.dot(h, w_ref[:, c * MXU_COLS:(c + 1) * MXU_COLS], preferred_element_type=F32)
        for hh in range(MXU_COLS // HEAD_DIM):
            cols = slice(c * MXU_COLS + hh * HEAD_DIM, c * MXU_COLS + (hh + 1) * HEAD_DIM)
            a = acc[:, hh * HEAD_DIM:(hh + 1) * HEAD_DIM]
            ms = jnp.mean(a * a, axis=-1, keepdims=True)
            f = jnp.where(is_norm_tile, lax.rsqrt(ms + EPS), 1.0)
            o_ref[:, cols] = a * f * hs_ref[:, cols]


def _norm_proj(x, norm_g, w, head_scale, n_norm_cols, tm, tn, row_orders=(1,)):
    m_rows, k_dim = x.shape
    n_cols = w.shape[1]
    assert m_rows % tm == 0 and n_cols % tn == 0 and n_norm_cols % tn == 0
    assert len(row_orders) == 1 or ATTN_WIDTH % tn == 0
    n_slabs = k_dim // HEAD_DIM
    in_specs = [pl.BlockSpec((tm, HEAD_DIM), lambda m, n, c=c: (m, c), pipeline_mode=pl.Buffered(1))
                for c in range(n_slabs)] + [
        pl.BlockSpec((1, k_dim), lambda m, n: (0, 0)),
        pl.BlockSpec((k_dim, tn), lambda m, n: (0, n)),
    ]
    args = [x] * n_slabs + [norm_g.reshape(1, k_dim), w]
    if n_norm_cols:
        in_specs.append(pl.BlockSpec((1, tn), lambda m, n: (0, n)))
        args.append(head_scale)
    return pl.pallas_call(
        functools.partial(_proj_kernel, n_norm_tiles=n_norm_cols // tn, row_orders=row_orders,
                          tiles_per_order=ATTN_WIDTH // tn, n_slabs=n_slabs),
        out_shape=jax.ShapeDtypeStruct((m_rows, n_cols), F32),
        grid=(m_rows // tm, n_cols // tn),
        in_specs=in_specs,
        out_specs=pl.BlockSpec((tm, tn), lambda m, n: (m, n)),
        scratch_shapes=[pltpu.VMEM((len(row_orders), tm, k_dim), BF16)],
        compiler_params=_params(("parallel", "arbitrary")),
        name="norm_proj",
    )(*args)


def _out_kernel(u_ref, w_ref, x_ref, y_ref):
    y_ref[...] = x_ref[...] + jnp.dot(u_ref[...].astype(BF16), w_ref[...], preferred_element_type=F32)


def _out_proj(u, w, x, tm):
    m_rows, k_dim = u.shape
    n_cols = w.shape[1]
    assert m_rows % tm == 0
    return pl.pallas_call(
        _out_kernel,
        out_shape=jax.ShapeDtypeStruct((m_rows, n_cols), F32),
        grid=(m_rows // tm,),
        in_specs=[
            pl.BlockSpec((tm, k_dim), lambda m: (m, 0)),
            pl.BlockSpec((k_dim, n_cols), lambda m: (0, 0)),
            pl.BlockSpec((tm, n_cols), lambda m: (m, 0)),
        ],
        out_specs=pl.BlockSpec((tm, n_cols), lambda m: (m, 0)),
        compiler_params=_params(("parallel",)),
        name="out_proj",
    )(u, w, x)


def _attn_prompt_kernel(q0, q1, q2, k0, k1, k2, v0, v1, v2, gate_ref, bias_ref, u_ref,
                        kx, vx, o_scr, l_scr):
    span = pl.program_id(2)
    q_refs = (q0, q1, q2)
    n_blk = SPAN // Q_BLK
    cur, prev = span % 2, 1 - span % 2

    @pl.when(span == 0)
    def _():
        kx[:, 1] = jnp.zeros((N_GROUPS, SPAN, HEAD_DIM), BF16)
        vx[:, 1] = jnp.zeros((N_GROUPS, SPAN, HEAD_DIM), BF16)

    for g, (k_ref, v_ref) in enumerate(((k0, v0), (k1, v1), (k2, v2))):
        kx[g, cur] = k_ref[...].astype(BF16)
        vx[g, cur] = v_ref[...].astype(BF16)

    first_span = span == 0

    def blk_rows(bi):
        return slice(bi * Q_BLK, (bi + 1) * Q_BLK)

    def prev_block(g, bi):
        per_residue = n_blk // DILATIONS[g]
        if bi % per_residue:
            return cur, bi - 1, None
        return prev, bi + per_residue - 1, first_span

    def token_rows(g, bi):
        d = DILATIONS[g]
        per_residue = n_blk // d
        return _residue_rows((bi % per_residue) * Q_BLK * d + bi // per_residue, Q_BLK, d)

    for g in range(N_GROUPS):
        each = range(n_blk)
        prevs = [prev_block(g, bi) for bi in each]
        q = [q_refs[g][blk_rows(bi), :].astype(BF16) for bi in each]
        k = [jnp.concatenate([kx[g, prevs[bi][0], blk_rows(prevs[bi][1]), :], kx[g, cur, blk_rows(bi), :]], axis=0)
             for bi in each]
        v = [jnp.concatenate([vx[g, prevs[bi][0], blk_rows(prevs[bi][1]), :], vx[g, cur, blk_rows(bi), :]], axis=0)
             for bi in each]
        s = [lax.dot_general(q[bi], k[bi], (((1,), (1,)), ((), ())), preferred_element_type=F32) + bias_ref[g]
             for bi in each]
        in_prev = lax.broadcasted_iota(jnp.int32, s[0].shape, 1) < Q_BLK
        s = [s[bi] if prevs[bi][2] is None else jnp.where(jnp.logical_and(prevs[bi][2], in_prev), NEG, s[bi])
             for bi in each]
        m = [jnp.max(s[bi], axis=-1, keepdims=True) for bi in each]
        p = [jnp.exp(s[bi] - m[bi]) for bi in each]
        l = [jnp.sum(p[bi], axis=-1, keepdims=True) for bi in each]
        o = [jnp.dot(p[bi].astype(BF16), v[bi], preferred_element_type=F32) / l[bi] for bi in each]
        for bi in each:
            o_scr[g, token_rows(g, bi), :] = o[bi]
            l_scr[g, token_rows(g, bi), :] = jnp.broadcast_to(m[bi] + jnp.log(l[bi]), (Q_BLK, HEAD_DIM))

    for c in range(n_blk):
        rows_ = blk_rows(c)
        l0, l1, l2 = l_scr[0, rows_, :], l_scr[1, rows_, :], l_scr[2, rows_, :]
        mx = jnp.maximum(jnp.maximum(l0, l1), l2)
        e0, e1, e2 = jnp.exp(l0 - mx), jnp.exp(l1 - mx), jnp.exp(l2 - mx)
        merged = (e0 * o_scr[0, rows_, :] + e1 * o_scr[1, rows_, :] + e2 * o_scr[2, rows_, :]) / (e0 + e1 + e2)
        gate = gate_ref[rows_, :]
        u_ref[rows_, :] = gate * jax.nn.sigmoid(gate) * merged


def _attn_prompt(proj, bias_band, batch, seq):
    n_span = seq // SPAN
    assert seq % SPAN == 0

    def col_spec(first_col_block):
        return pl.BlockSpec((SPAN, HEAD_DIM),
                            lambda b, h, s, c=first_col_block: (b * n_span + s, c + h))

    head_blocks = QKV_WIDTH // HEAD_DIM
    in_specs = (
        [col_spec(g * HEADS_PER_GROUP) for g in range(N_GROUPS)]
        + [col_spec(head_blocks + g * HEADS_PER_GROUP) for g in range(N_GROUPS)]
        + [col_spec(2 * head_blocks + g * HEADS_PER_GROUP) for g in range(N_GROUPS)]
        + [col_spec(3 * head_blocks)]
        + [pl.BlockSpec((None, N_GROUPS, Q_BLK, 2 * Q_BLK), lambda b, h, s: (h, 0, 0, 0))]
    )
    return pl.pallas_call(
        _attn_prompt_kernel,
        out_shape=jax.ShapeDtypeStruct((batch * seq, ATTN_WIDTH), F32),
        grid=(batch, HEADS_PER_GROUP, n_span),
        in_specs=in_specs,
        out_specs=pl.BlockSpec((SPAN, HEAD_DIM), lambda b, h, s: (b * n_span + s, h)),
        scratch_shapes=[
            pltpu.VMEM((N_GROUPS, 2, SPAN, HEAD_DIM), BF16),
            pltpu.VMEM((N_GROUPS, 2, SPAN, HEAD_DIM), BF16),
            pltpu.VMEM((N_GROUPS, SPAN, HEAD_DIM), F32),
            pltpu.VMEM((N_GROUPS, SPAN, HEAD_DIM), F32),
        ],
        compiler_params=_params(("parallel", "parallel", "arbitrary")),
        name="attn_prompt",
    )(*([proj] * 10), bias_band)


def _decode_kernel(q_ref, kn_ref, vn_ref, gate_ref, bias_ref, c0_ref, c1_ref, c2_ref,
                   u_ref, o0_ref, o1_ref, o2_ref, s_scr):
    cache_refs = (c0_ref, c1_ref, c2_ref)
    n_t = q_ref.shape[0]
    outs, lses = [], []
    for g in range(N_GROUPS):
        d = DILATIONS[g]
        heads = slice(g * HEADS_PER_GROUP, (g + 1) * HEADS_PER_GROUP)
        c_ref = cache_refs[g]
        n_cache = N_KEYS - (n_t - 1) // d
        q_t = [q_ref[t, heads, :] for t in range(n_t)]

        def key_tile(kv, t, i, g=g, d=d, c_ref=c_ref, heads=heads):
            row = t + i * d
            if row < WINDOWS[g]:
                return c_ref[row // d, row % d, kv]
            return (kn_ref, vn_ref)[kv][row - WINDOWS[g], heads, :]

        def store_scores(keys, tiles, g=g):
            prod = jnp.concatenate([q_t[t] * tiles[ii][t] for ii in range(len(keys)) for t in range(n_t)], axis=0)
            sums = jnp.dot(prod.astype(BF16), jnp.ones((HEAD_DIM, HEAD_DIM), BF16), preferred_element_type=F32)
            for ii, i in enumerate(keys):
                for t in range(n_t):
                    row0 = (ii * n_t + t) * HEADS_PER_GROUP
                    s_scr[t, i] = sums[row0:row0 + HEADS_PER_GROUP] + bias_ref[g, i]

        def scores_body(it, carry, d=d, c_ref=c_ref, store_scores=store_scores):
            keys = [it * SCORE_KEYS + ii for ii in range(SCORE_KEYS)]
            store_scores(keys, [[c_ref[i + t // d, t % d, 0] for t in range(n_t)] for i in keys])
            return carry
        lax.fori_loop(0, n_cache // SCORE_KEYS, scores_body, 0, unroll=SCORE_UNROLL)
        rest = list(range(n_cache // SCORE_KEYS * SCORE_KEYS, N_KEYS + 1))
        store_scores(rest, [[key_tile(0, t, i) for t in range(n_t)] for i in rest])

        m_t, l_t = [], []
        for t in range(n_t):
            sv = s_scr[t, 0:N_KEYS + 1]
            m = jnp.max(sv, axis=0)
            p = jnp.exp(sv - m)
            s_scr[t, 0:N_KEYS + 1] = p
            m_t.append(m)
            l_t.append(jnp.sum(p, axis=0))

        def pv_body(i, accs, d=d, c_ref=c_ref):
            return tuple(accs[t] + s_scr[t, i] * c_ref[i + t // d, t % d, 1] for t in range(n_t))
        accs = lax.fori_loop(0, n_cache, pv_body,
                             tuple(jnp.zeros((HEADS_PER_GROUP, HEAD_DIM), F32) for _ in range(n_t)))
        accs = list(accs)
        for i in range(n_cache, N_KEYS + 1):
            for t in range(n_t):
                accs[t] = accs[t] + s_scr[t, i] * key_tile(1, t, i)
        outs.append([accs[t] / l_t[t] for t in range(n_t)])
        lses.append([m_t[t] + jnp.log(l_t[t]) for t in range(n_t)])

    for t in range(n_t):
        l0, l1, l2 = lses[0][t], lses[1][t], lses[2][t]
        mx = jnp.maximum(jnp.maximum(l0, l1), l2)
        e0, e1, e2 = jnp.exp(l0 - mx), jnp.exp(l1 - mx), jnp.exp(l2 - mx)
        merged = (e0 * outs[0][t] + e1 * outs[1][t] + e2 * outs[2][t]) / (e0 + e1 + e2)
        gate = gate_ref[t]
        u_ref[t] = gate * jax.nn.sigmoid(gate) * merged

    for g, o_ref in enumerate((o0_ref, o1_ref, o2_ref)):
        heads = slice(g * HEADS_PER_GROUP, (g + 1) * HEADS_PER_GROUP)
        o_ref[:, 0] = kn_ref[:, heads, :]
        o_ref[:, 1] = vn_ref[:, heads, :]


def _decode_attn(layer, q_s, k_s, v_s, gate_s, bias_dec, caches):
    bd, n_t = q_s.shape[:2]
    assert n_t == SUBLANES
    views, view_specs = [], []
    for g, cache in enumerate(caches):
        d = DILATIONS[g]
        n_res = min(d, n_t)
        views.append(cache.reshape(cache.shape[0], bd, N_KEYS, d, 2, HEADS_PER_GROUP, HEAD_DIM))
        view_specs.append(pl.BlockSpec((None, None, N_KEYS, n_res, 2, HEADS_PER_GROUP, HEAD_DIM),
                                       lambda b: (layer, b, 0, 0, 0, 0, 0)))
    tok_spec = pl.BlockSpec((None, n_t, N_HEADS, HEAD_DIM), lambda b: (b, 0, 0, 0))
    u_spec = pl.BlockSpec((None, n_t, HEADS_PER_GROUP, HEAD_DIM), lambda b: (b, 0, 0, 0))
    new_rows_shape = (bd, n_t, 2, HEADS_PER_GROUP, HEAD_DIM)
    new_rows_spec = pl.BlockSpec((None,) + new_rows_shape[1:], lambda b: (b, 0, 0, 0, 0))
    res = pl.pallas_call(
        _decode_kernel,
        out_shape=[jax.ShapeDtypeStruct(gate_s.shape, F32)]
        + [jax.ShapeDtypeStruct(new_rows_shape, c.dtype) for c in caches],
        grid=(bd,),
        in_specs=[tok_spec, tok_spec, tok_spec, u_spec,
                  pl.BlockSpec(bias_dec.shape, lambda b: (0, 0, 0, 0))] + view_specs,
        out_specs=[u_spec] + [new_rows_spec] * N_GROUPS,
        scratch_shapes=[pltpu.VMEM((n_t, N_KEYS + SUBLANES, HEADS_PER_GROUP, HEAD_DIM), F32)],
        compiler_params=_params(("parallel",)),
        name="decode_attn",
    )(q_s, k_s, v_s, gate_s, bias_dec, *views)
    return res[0], tuple(res[1:])


def _cache_update_kernel(cur_ref, next_ref, new_ref, o_ref):
    rows_, n_new = cur_ref.shape[0], new_ref.shape[0]
    o_ref[0:rows_ - n_new] = cur_ref[n_new:rows_]
    last = pl.program_id(1) == pl.num_programs(1) - 1

    @pl.when(jnp.logical_not(last))
    def _():
        o_ref[rows_ - n_new:rows_] = next_ref[...]

    @pl.when(last)
    def _():
        o_ref[rows_ - n_new:rows_] = new_ref[...]


def _cache_update(cache, new_rows):
    n_layers, batch, window = cache.shape[:3]
    n_new = new_rows.shape[2]
    chunk = min(window, CACHE_CHUNK_ROWS)
    assert window % chunk == 0 and chunk % n_new == 0
    tile = cache.shape[3:]
    zeros = (0,) * len(tile)
    next_blocks = chunk // n_new
    last_block = window // n_new - 1
    return pl.pallas_call(
        _cache_update_kernel,
        out_shape=jax.ShapeDtypeStruct(cache.shape, cache.dtype),
        grid=(n_layers * batch, window // chunk),
        in_specs=[
            pl.BlockSpec((None, None, chunk) + tile, lambda i, c: (i // batch, i % batch, c) + zeros),
            pl.BlockSpec((None, None, n_new) + tile,
                         lambda i, c: (i // batch, i % batch, jnp.minimum((c + 1) * next_blocks, last_block))
                         + zeros),
            pl.BlockSpec((None, None, n_new) + tile, lambda i, c: (i // batch, i % batch, 0) + zeros),
        ],
        out_specs=pl.BlockSpec((None, None, chunk) + tile, lambda i, c: (i // batch, i % batch, c) + zeros),
        compiler_params=_params(("parallel", "arbitrary")),
        name="cache_update",
    )(cache, cache, new_rows)


def _scan_rows(a, b, h_prev):
    n_tiles = a.shape[0] // SUBLANES
    a3 = a.reshape(n_tiles, SUBLANES, a.shape[1])
    b3 = b.reshape(n_tiles, SUBLANES, b.shape[1])
    row = lax.broadcasted_iota(jnp.int32, a3.shape, 1)
    shift = 1
    while shift < SUBLANES:
        a_sh = pltpu.roll(a3, shift, axis=1)
        b_sh = pltpu.roll(b3, shift, axis=1)
        keep = row >= shift
        b3 = jnp.where(keep, a3 * b_sh + b3, b3)
        a3 = jnp.where(keep, a3 * a_sh, a3)
        shift *= 2
    tiles = []
    carry = h_prev
    for k in range(n_tiles):
        h_k = b3[k] + a3[k] * carry
        carry = h_k[SUBLANES - 1:SUBLANES]
        tiles.append(h_k)
    return jnp.concatenate(tiles, axis=0), carry


def _rnn_kernel(x_ref, ng_ref, win_ref, conv0_ref, h0_ref, cw_ref, cb_ref, gaw_ref, gab_ref,
                gxw_ref, gxb_ref, lam_ref, wout_ref, y_ref, hlast_ref, clast_ref,
                xe_scr, gate_scr, h_scr, u_scr):
    c = pl.program_id(1)
    tl = x_ref.shape[0]
    pad = SUBLANES

    @pl.when(c == 0)
    def _():
        xe_scr[pad - (CONV_W - 1):pad] = conv0_ref[...]
        h_scr[...] = h0_ref[...]

    @pl.when(c > 0)
    def _():
        xe_scr[0:pad] = xe_scr[tl:tl + pad]

    x = x_ref[...]
    ms = jnp.mean(x * x, axis=-1, keepdims=True)
    hn = (x * lax.rsqrt(ms + EPS) * ng_ref[...]).astype(BF16)
    xe_scr[pad:pad + tl] = jnp.dot(hn, win_ref[:, :D_RNN], preferred_element_type=F32)
    gate_scr[...] = jnp.dot(hn, win_ref[:, D_RNN:], preferred_element_type=F32)
    clast_ref[...] = xe_scr[pad + tl - (CONV_W - 1):pad + tl]

    for j in range(RNN_BLOCKS):
        cols = slice(j * RNN_BLOCK_W, (j + 1) * RNN_BLOCK_W)
        xc = cb_ref[:, cols]
        for k in range(CONV_W):
            start = pad - (CONV_W - 1) + k
            xc = xc + cw_ref[k:k + 1, cols] * xe_scr[start:start + tl, cols]
        xcb = xc.astype(BF16)
        r = jax.nn.sigmoid(jnp.dot(xcb, gaw_ref[j], preferred_element_type=F32) + gab_ref[:, cols])
        ig = jax.nn.sigmoid(jnp.dot(xcb, gxw_ref[j], preferred_element_type=F32) + gxb_ref[:, cols])
        neg_lam = -lam_ref[:, cols]
        softplus = jnp.maximum(neg_lam, 0.0) + jnp.log1p(jnp.exp(-jnp.abs(neg_lam)))
        log_a = -LRU_C * r * softplus
        a = jnp.exp(log_a)
        bterm = jnp.sqrt(-jnp.tanh(log_a) * (a * a + 1.0)) * (ig * xc)
        h, h_end = _scan_rows(a, bterm, h_scr[:, cols])
        h_scr[:, cols] = h_end
        gate = gate_scr[:, cols]
        u_scr[:, cols] = gate * jax.nn.sigmoid(gate) * h

    hlast_ref[...] = h_scr[...]
    y_ref[...] = x_ref[...] + jnp.dot(u_scr[...].astype(BF16), wout_ref[...], preferred_element_type=F32)


def _rnn_layer(x, norm_g, w_in, conv0, h0, conv_w, conv_b, ga_w, ga_b, gx_w, gx_b, lam, w_out, batch, seq, tl):
    n_chunks = seq // tl
    assert seq % tl == 0 and tl % SUBLANES == 0
    row = lambda b, c: (b * n_chunks + c, 0)
    vec = lambda a: a.reshape(1, -1)
    full = lambda shape: pl.BlockSpec(shape, lambda b, c: (0,) * len(shape))
    per_batch = lambda rows_: pl.BlockSpec((None, rows_, D_RNN), lambda b, c: (b, 0, 0))
    y, h_last, c_last = pl.pallas_call(
        _rnn_kernel,
        out_shape=[jax.ShapeDtypeStruct((batch * seq, D_MODEL), F32),
                   jax.ShapeDtypeStruct((batch, 1, D_RNN), F32),
                   jax.ShapeDtypeStruct((batch, CONV_W - 1, D_RNN), F32)],
        grid=(batch, n_chunks),
        in_specs=[
            pl.BlockSpec((tl, D_MODEL), row),
            full((1, D_MODEL)),
            full((D_MODEL, 2 * D_RNN)),
            per_batch(CONV_W - 1),
            per_batch(1),
            full((CONV_W, D_RNN)),
            full((1, D_RNN)),
            full((RNN_BLOCKS, RNN_BLOCK_W, RNN_BLOCK_W)),
            full((1, D_RNN)),
            full((RNN_BLOCKS, RNN_BLOCK_W, RNN_BLOCK_W)),
            full((1, D_RNN)),
            full((1, D_RNN)),
            full((D_RNN, D_MODEL)),
        ],
        out_specs=[pl.BlockSpec((tl, D_MODEL), row), per_batch(1), per_batch(CONV_W - 1)],
        scratch_shapes=[
            pltpu.VMEM((tl + SUBLANES, D_RNN), F32),
            pltpu.VMEM((tl, D_RNN), F32),
            pltpu.VMEM((1, D_RNN), F32),
            pltpu.VMEM((tl, D_RNN), F32),
        ],
        compiler_params=_params(("parallel", "arbitrary")),
        name="rnn_layer",
    )(x, vec(norm_g), w_in, conv0, h0.reshape(batch, 1, D_RNN), conv_w, vec(conv_b), ga_w, vec(ga_b),
      gx_w, vec(gx_b), vec(lam), w_out)
    return y, h_last.reshape(batch, D_RNN), c_last


def _t5_bucket(dist):
    n = jnp.maximum(dist, 0)
    max_exact = N_BUCKETS // 2
    nf = jnp.maximum(n, 1).astype(F32)
    large = max_exact + (jnp.log(nf / max_exact) / math.log(MAX_DISTANCE / max_exact)
                         * (N_BUCKETS - max_exact)).astype(jnp.int32)
    large = jnp.minimum(large, N_BUCKETS - 1)
    return jnp.where(n < max_exact, n, large)


def _bias_tables(rel_bias):
    steps = jnp.arange(N_KEYS + 1)
    period = 3 * Q_BLK
    bands, decs = [], []
    for g in range(N_GROUPS):
        heads = slice(g * HEADS_PER_GROUP, (g + 1) * HEADS_PER_GROUP)
        per_step = rel_bias[_t5_bucket(steps * DILATIONS[g])][:, heads]
        row = jnp.concatenate([per_step[::-1].T,
                               jnp.full((HEADS_PER_GROUP, period - (N_KEYS + 1)), NEG, F32)], axis=1)
        band = jnp.tile(row, (1, Q_BLK))[:, :Q_BLK * (period - 1)]
        bands.append(band.reshape(HEADS_PER_GROUP, Q_BLK, period - 1)[:, :, :2 * Q_BLK])
        decs.append(jnp.broadcast_to(per_step[::-1][:, :, None],
                                     (N_KEYS + 1, HEADS_PER_GROUP, HEAD_DIM)))
    return jnp.stack(bands, axis=1).astype(F32), jnp.stack(decs).astype(F32)


def kernel(x_prompt, x_sample, cache_kv_w128, cache_kv_w512, cache_kv_w2048, state_rglru_h,
           state_rglru_conv, attn_norm, attn_w_in, attn_q_norm, attn_k_norm, attn_w_out, rel_bias,
           rnn_norm, rnn_w_in, rnn_conv_w, rnn_conv_b, rnn_gate_a_w, rnn_gate_a_b, rnn_gate_x_w,
           rnn_gate_x_b, rnn_lambda, rnn_w_out):
    batch, seq, _ = x_prompt.shape
    bd, n_t, _ = x_sample.shape
    depth = attn_norm.shape[0] + rnn_norm.shape[0]
    caches = (cache_kv_w128, cache_kv_w512, cache_kv_w2048)

    yp = x_prompt.reshape(batch * seq, D_MODEL)
    ys = x_sample.reshape(bd * n_t, D_MODEL)
    tm_p, tm_s = 1024, bd * n_t

    bias_band, bias_dec = _bias_tables(rel_bias)
    kv_p = [[] for _ in range(N_GROUPS)]
    kv_new = [[] for _ in range(N_GROUPS)]
    h_p, h_s, c_p, c_s = [], [], [], []
    for i in range(depth):
        li = i // 2
        if i % 2 == 0:
            w_in = attn_w_in[li].astype(BF16)
            w_out = attn_w_out[li].astype(BF16)
            head_scale = jnp.concatenate([
                jnp.tile(attn_q_norm[li] * (HEAD_DIM ** -0.5), N_HEADS),
                jnp.tile(attn_k_norm[li], N_HEADS),
                jnp.ones((QKV_WIDTH + ATTN_WIDTH,), F32)]).reshape(1, 3 * QKV_WIDTH + ATTN_WIDTH)
            proj_p = _norm_proj(yp, attn_norm[li], w_in, head_scale, 2 * QKV_WIDTH, SPAN, 1024,
                                row_orders=DILATIONS)
            proj_s = _norm_proj(ys, attn_norm[li], w_in, head_scale, 2 * QKV_WIDTH, tm_s, 1024)

            u_p = _attn_prompt(proj_p, bias_band, batch, seq)
            yp = _out_proj(u_p, w_out, yp, tm_p)

            tok = lambda a, heads: a.reshape(bd, n_t, heads, HEAD_DIM)
            q_s = tok(proj_s[:, :QKV_WIDTH], N_HEADS)
            k_s = tok(proj_s[:, QKV_WIDTH:2 * QKV_WIDTH], N_HEADS)
            v_s = tok(proj_s[:, 2 * QKV_WIDTH:3 * QKV_WIDTH], N_HEADS)
            gate_s = tok(proj_s[:, 3 * QKV_WIDTH:], HEADS_PER_GROUP)
            u_s, new_rows = _decode_attn(li, q_s, k_s, v_s, gate_s, bias_dec, caches)
            for g in range(N_GROUPS):
                kv_new[g].append(new_rows[g])
            ys = _out_proj(u_s.reshape(bd * n_t, ATTN_WIDTH), w_out, ys, tm_s)

            for g in range(N_GROUPS):
                d = DILATIONS[g]
                keep = min(WINDOWS[g], seq)
                assert keep <= SPAN and keep % d == 0
                k_cols = slice(QKV_WIDTH + g * ATTN_WIDTH, QKV_WIDTH + (g + 1) * ATTN_WIDTH)
                v_cols = slice(2 * QKV_WIDTH + g * ATTN_WIDTH, 2 * QKV_WIDTH + (g + 1) * ATTN_WIDTH)

                def newest(b, cols, d=d, keep=keep):
                    last_span = proj_p[(b + 1) * seq - SPAN:(b + 1) * seq, cols]
                    by_residue = last_span.reshape(d, SPAN // d, HEADS_PER_GROUP, HEAD_DIM)
                    kept = by_residue[:, (SPAN - keep) // d:]
                    return kept.transpose(1, 0, 2, 3).reshape(keep, HEADS_PER_GROUP, HEAD_DIM)

                kv_p[g].append(jnp.stack([jnp.stack([newest(b, k_cols), newest(b, v_cols)], axis=1)
                                          for b in range(batch)]))
        else:
            w_in = rnn_w_in[li].astype(BF16)
            rnn_args = (rnn_conv_w[li], rnn_conv_b[li], rnn_gate_a_w[li].astype(BF16), rnn_gate_a_b[li].reshape(-1),
                        rnn_gate_x_w[li].astype(BF16), rnn_gate_x_b[li].reshape(-1), rnn_lambda[li],
                        rnn_w_out[li].astype(BF16))
            yp, hp, cp = _rnn_layer(yp, rnn_norm[li], w_in, jnp.zeros((batch, CONV_W - 1, D_RNN), F32),
                                    jnp.zeros((batch, D_RNN), F32), *rnn_args, batch, seq, RNN_CHUNK)
            ys, hs, cs = _rnn_layer(ys, rnn_norm[li], w_in, state_rglru_conv[li], state_rglru_h[li], *rnn_args,
                                    bd, n_t, n_t)
            h_p.append(hp)
            h_s.append(hs)
            c_p.append(cp)
            c_s.append(cs)

    shifted = [_cache_update(caches[g], jnp.stack(kv_new[g])) for g in range(N_GROUPS)]
    return (yp.reshape(batch, seq, D_MODEL), ys.reshape(bd, n_t, D_MODEL),
            jnp.stack(kv_p[0]), shifted[0],
            jnp.stack(kv_p[1]), shifted[1],
            jnp.stack(kv_p[2]), shifted[2],
            jnp.stack(h_p), jnp.stack(h_s),
            jnp.stack(c_p), jnp.stack(c_s))
```

```python
import functools
import math

import jax
import jax.numpy as jnp
from jax import lax
from jax.experimental import pallas as pl
from jax.experimental.pallas import tpu as pltpu

F32 = jnp.float32
BF16 = jnp.bfloat16

D_MODEL = 1024
N_GROUPS = 3
WINDOWS = (128, 512, 2048)
DILATIONS = (1, 4, 16)
HEADS_PER_GROUP = 8
HEAD_DIM = 128
N_HEADS = N_GROUPS * HEADS_PER_GROUP
QKV_WIDTH = N_HEADS * HEAD_DIM
ATTN_WIDTH = HEADS_PER_GROUP * HEAD_DIM
Q_BLK = 128
N_KEYS = Q_BLK
SPAN = Q_BLK * DILATIONS[-1]
RNN_CHUNK = 256
CACHE_CHUNK_ROWS = 512
SCORE_KEYS = 8
SCORE_UNROLL = 4
N_BUCKETS = 32
MAX_DISTANCE = 2048
D_RNN = 1280
RNN_BLOCKS = 10
RNN_BLOCK_W = D_RNN // RNN_BLOCKS
CONV_W = 4
LRU_C = 8.0
EPS = 1e-6
NEG = -1e30

SUBLANES = 8
MXU_COLS = 256
VMEM_LIMIT = 52 * 1024 * 1024


def _params(semantics):
    return pltpu.CompilerParams(dimension_semantics=semantics, vmem_limit_bytes=VMEM_LIMIT)


def _residue_rows(start, size, stride):
    return pl.ds(start, size) if stride == 1 else pl.ds(start, size, stride=stride)


def _proj_kernel(*refs, n_norm_tiles, row_orders, tiles_per_order, n_slabs):
    x_refs, (g_ref, w_ref), rest = refs[:n_slabs], refs[n_slabs:n_slabs + 2], refs[n_slabs + 2:]
    hs_ref, o_ref, h_scr = rest
    n = pl.program_id(1)
    tm = x_refs[0].shape[0]
    k_dim = n_slabs * HEAD_DIM

    @pl.when(n == 0)
    def _():
        for i, d in enumerate(row_orders):
            for r in range(d):
                xs = [x_ref[_residue_rows(r, tm // d, d), :] for x_ref in x_refs]
                ms = jnp.sum(sum(x * x for x in xs), axis=-1, keepdims=True) * (1.0 / k_dim)
                inv = lax.rsqrt(ms + EPS)
                for c, x in enumerate(xs):
                    cols = slice(c * HEAD_DIM, (c + 1) * HEAD_DIM)
                    h_scr[i, r * (tm // d):(r + 1) * (tm // d), cols] = (x * inv * g_ref[:, cols]).astype(BF16)

    if len(row_orders) == 1:
        h = h_scr[0]
    else:
        order = jnp.where(n < tiles_per_order * len(row_orders) * 3, (n // tiles_per_order) % len(row_orders), 0)
        h = h_scr[order]
    tn = w_ref.shape[1]

    def store(head, value):
        if len(o_ref.shape) == 3:
            o_ref[:, head, :] = value
        else:
            o_ref[:, head * HEAD_DIM:(head + 1) * HEAD_DIM] = value

    @pl.when(n < n_norm_tiles)
    def _():
        for c in range(tn // MXU_COLS):
            acc = jnp.dot(h, w_ref[:, c * MXU_COLS:(c + 1) * MXU_COLS], preferred_element_type=F32)
            for hh in range(MXU_COLS // HEAD_DIM):
                head = c * (MXU_COLS // HEAD_DIM) + hh
                a = acc[:, hh * HEAD_DIM:(hh + 1) * HEAD_DIM]
                ms = jnp.mean(a * a, axis=-1, keepdims=True)
                store(head, a * lax.rsqrt(ms + EPS) * hs_ref[:, head * HEAD_DIM:(head + 1) * HEAD_DIM])

    @pl.when(n >= n_norm_tiles)
    def _():
        acc = jnp.dot(h, w_ref[...], preferred_element_type=F32)
        if len(o_ref.shape) == 3:
            for head in range(tn // HEAD_DIM):
                store(head, acc[:, head * HEAD_DIM:(head + 1) * HEAD_DIM])
        else:
            o_ref[...] = acc


def _norm_proj(x, norm_g, w, head_scale, tm, tn, row_orders=(1,), split_heads=False):
    m_rows, k_dim = x.shape
    n_cols = w.shape[1]
    n_norm_cols = head_scale.shape[1]
    assert m_rows % tm == 0 and n_cols % tn == 0 and n_norm_cols % tn == 0 and tn % MXU_COLS == 0
    assert len(row_orders) == 1 or ATTN_WIDTH % tn == 0
    n_slabs = k_dim // HEAD_DIM
    last_norm_tile = n_norm_cols // tn - 1
    if split_heads:
        out_shape = (m_rows, n_cols // HEAD_DIM, HEAD_DIM)
        out_spec = pl.BlockSpec((tm, tn // HEAD_DIM, HEAD_DIM), lambda m, n: (m, n, 0))
    else:
        out_shape = (m_rows, n_cols)
        out_spec = pl.BlockSpec((tm, tn), lambda m, n: (m, n))
    in_specs = [pl.BlockSpec((tm, HEAD_DIM), lambda m, n, c=c: (m, c), pipeline_mode=pl.Buffered(1))
                for c in range(n_slabs)] + [
        pl.BlockSpec((1, k_dim), lambda m, n: (0, 0)),
        pl.BlockSpec((k_dim, tn), lambda m, n: (0, n)),
        pl.BlockSpec((1, tn), lambda m, n: (0, jnp.minimum(n, last_norm_tile))),
    ]
    args = [x] * n_slabs + [norm_g.reshape(1, k_dim), w, head_scale]
    return pl.pallas_call(
        functools.partial(_proj_kernel, n_norm_tiles=n_norm_cols // tn, row_orders=row_orders,
                          tiles_per_order=ATTN_WIDTH // tn, n_slabs=n_slabs),
        out_shape=jax.ShapeDtypeStruct(out_shape, F32),
        grid=(m_rows // tm, n_cols // tn),
        in_specs=in_specs,
        out_specs=out_spec,
        scratch_shapes=[pltpu.VMEM((len(row_orders), tm, k_dim), BF16)],
        compiler_params=_params(("parallel", "arbitrary")),
        name="norm_proj",
    )(*args)


def _out_kernel(u_ref, w_ref, x_ref, y_ref):
    y_ref[...] = x_ref[...] + jnp.dot(u_ref[...].astype(BF16), w_ref[...], preferred_element_type=F32)


def _out_proj(u, w, x, tm):
    m_rows, k_dim = u.shape
    n_cols = w.shape[1]
    assert m_rows % tm == 0
    return pl.pallas_call(
        _out_kernel,
        out_shape=jax.ShapeDtypeStruct((m_rows, n_cols), F32),
        grid=(m_rows // tm,),
        in_specs=[
            pl.BlockSpec((tm, k_dim), lambda m: (m, 0)),
            pl.BlockSpec((k_dim, n_cols), lambda m: (0, 0)),
            pl.BlockSpec((tm, n_cols), lambda m: (m, 0)),
        ],
        out_specs=pl.BlockSpec((tm, n_cols), lambda m: (m, 0)),
        compiler_params=_params(("parallel",)),
        name="out_proj",
    )(u, w, x)


def _attn_prompt_kernel(q0, q1, q2, k0, k1, k2, v0, v1, v2, gate_ref, bias_ref, u_ref,
                        kx, vx, o_scr, l_scr):
    span = pl.program_id(2)
    q_refs = (q0, q1, q2)
    n_blk = SPAN // Q_BLK
    cur, prev = span % 2, 1 - span % 2

    @pl.when(span == 0)
    def _():
        kx[:, 1] = jnp.zeros((N_GROUPS, SPAN, HEAD_DIM), BF16)
        vx[:, 1] = jnp.zeros((N_GROUPS, SPAN, HEAD_DIM), BF16)

    for g, (k_ref, v_ref) in enumerate(((k0, v0), (k1, v1), (k2, v2))):
        kx[g, cur] = k_ref[...].astype(BF16)
        vx[g, cur] = v_ref[...].astype(BF16)

    first_span = span == 0

    def blk_rows(bi):
        return slice(bi * Q_BLK, (bi + 1) * Q_BLK)

    def prev_block(g, bi):
        per_residue = n_blk // DILATIONS[g]
        if bi % per_residue:
            return cur, bi - 1, None
        return prev, bi + per_residue - 1, first_span

    def token_rows(g, bi):
        d = DILATIONS[g]
        per_residue = n_blk // d
        return _residue_rows((bi % per_residue) * Q_BLK * d + bi // per_residue, Q_BLK, d)

    for g in range(N_GROUPS):
        each = range(n_blk)
        prevs = [prev_block(g, bi) for bi in each]
        q = [q_refs[g][blk_rows(bi), :].astype(BF16) for bi in each]
        k = [jnp.concatenate([kx[g, prevs[bi][0], blk_rows(prevs[bi][1]), :], kx[g, cur, blk_rows(bi), :]], axis=0)
             for bi in each]
        v = [jnp.concatenate([vx[g, prevs[bi][0], blk_rows(prevs[bi][1]), :], vx[g, cur, blk_rows(bi), :]], axis=0)
             for bi in each]
        s = [lax.dot_general(q[bi], k[bi], (((1,), (1,)), ((), ())), preferred_element_type=F32) + bias_ref[g]
             for bi in each]
        in_prev = lax.broadcasted_iota(jnp.int32, s[0].shape, 1) < Q_BLK
        s = [s[bi] if prevs[bi][2] is None else jnp.where(jnp.logical_and(prevs[bi][2], in_prev), NEG, s[bi])
             for bi in each]
        m = [jnp.max(s[bi], axis=-1, keepdims=True) for bi in each]
        p = [jnp.exp(s[bi] - m[bi]) for bi in each]
        l = [jnp.sum(p[bi], axis=-1, keepdims=True) for bi in each]
        o = [jnp.dot(p[bi].astype(BF16), v[bi], preferred_element_type=F32) / l[bi] for bi in each]
        for bi in each:
            o_scr[g, token_rows(g, bi), :] = o[bi]
            l_scr[g, token_rows(g, bi), :] = jnp.broadcast_to(m[bi] + jnp.log(l[bi]), (Q_BLK, HEAD_DIM))

    for c in range(n_blk):
        rows_ = blk_rows(c)
        l0, l1, l2 = l_scr[0, rows_, :], l_scr[1, rows_, :], l_scr[2, rows_, :]
        mx = jnp.maximum(jnp.maximum(l0, l1), l2)
        e0, e1, e2 = jnp.exp(l0 - mx), jnp.exp(l1 - mx), jnp.exp(l2 - mx)
        merged = (e0 * o_scr[0, rows_, :] + e1 * o_scr[1, rows_, :] + e2 * o_scr[2, rows_, :]) / (e0 + e1 + e2)
        gate = gate_ref[rows_, :]
        u_ref[rows_, :] = gate * jax.nn.sigmoid(gate) * merged


def _attn_prompt(proj, bias_band, batch, seq):
    n_span = seq // SPAN
    assert seq % SPAN == 0

    def col_spec(first_col_block):
        return pl.BlockSpec((SPAN, HEAD_DIM),
                            lambda b, h, s, c=first_col_block: (b * n_span + s, c + h))

    head_blocks = QKV_WIDTH // HEAD_DIM
    in_specs = (
        [col_spec(g * HEADS_PER_GROUP) for g in range(N_GROUPS)]
        + [col_spec(head_blocks + g * HEADS_PER_GROUP) for g in range(N_GROUPS)]
        + [col_spec(2 * head_blocks + g * HEADS_PER_GROUP) for g in range(N_GROUPS)]
        + [col_spec(3 * head_blocks)]
        + [pl.BlockSpec((None, N_GROUPS, Q_BLK, 2 * Q_BLK), lambda b, h, s: (h, 0, 0, 0))]
    )
    return pl.pallas_call(
        _attn_prompt_kernel,
        out_shape=jax.ShapeDtypeStruct((batch * seq, ATTN_WIDTH), F32),
        grid=(batch, HEADS_PER_GROUP, n_span),
        in_specs=in_specs,
        out_specs=pl.BlockSpec((SPAN, HEAD_DIM), lambda b, h, s: (b * n_span + s, h)),
        scratch_shapes=[
            pltpu.VMEM((N_GROUPS, 2, SPAN, HEAD_DIM), BF16),
            pltpu.VMEM((N_GROUPS, 2, SPAN, HEAD_DIM), BF16),
            pltpu.VMEM((N_GROUPS, SPAN, HEAD_DIM), F32),
            pltpu.VMEM((N_GROUPS, SPAN, HEAD_DIM), F32),
        ],
        compiler_params=_params(("parallel", "parallel", "arbitrary")),
        name="attn_prompt",
    )(*([proj] * 10), bias_band)


def _decode_kernel(q_ref, kn_ref, vn_ref, gate_ref, bias_ref, c0_ref, c1_ref, c2_ref,
                   u_ref, o0_ref, o1_ref, o2_ref, s_scr):
    cache_refs = (c0_ref, c1_ref, c2_ref)
    n_t = q_ref.shape[0]
    outs, lses = [], []
    for g in range(N_GROUPS):
        d = DILATIONS[g]
        heads = slice(g * HEADS_PER_GROUP, (g + 1) * HEADS_PER_GROUP)
        c_ref = cache_refs[g]
        n_cache = N_KEYS - (n_t - 1) // d
        q_t = [q_ref[t, heads, :] for t in range(n_t)]

        def key_tile(kv, t, i, g=g, d=d, c_ref=c_ref, heads=heads):
            row = t + i * d
            if row < WINDOWS[g]:
                return c_ref[row // d, row % d, kv]
            return (kn_ref, vn_ref)[kv][row - WINDOWS[g], heads, :]

        def store_scores(keys, tiles, g=g):
            prod = jnp.concatenate([q_t[t] * tiles[ii][t] for ii in range(len(keys)) for t in range(n_t)], axis=0)
            sums = jnp.dot(prod.astype(BF16), jnp.ones((HEAD_DIM, HEAD_DIM), BF16), preferred_element_type=F32)
            for ii, i in enumerate(keys):
                for t in range(n_t):
                    row0 = (ii * n_t + t) * HEADS_PER_GROUP
                    s_scr[t, i] = sums[row0:row0 + HEADS_PER_GROUP] + bias_ref[g, i]

        def scores_body(it, carry, d=d, c_ref=c_ref, store_scores=store_scores):
            keys = [it * SCORE_KEYS + ii for ii in range(SCORE_KEYS)]
            store_scores(keys, [[c_ref[i + t // d, t % d, 0] for t in range(n_t)] for i in keys])
            return carry
        lax.fori_loop(0, n_cache // SCORE_KEYS, scores_body, 0, unroll=SCORE_UNROLL)
        rest = list(range(n_cache // SCORE_KEYS * SCORE_KEYS, N_KEYS + 1))
        store_scores(rest, [[key_tile(0, t, i) for t in range(n_t)] for i in rest])

        m_t, l_t = [], []
        for t in range(n_t):
            sv = s_scr[t, 0:N_KEYS + 1]
            m = jnp.max(sv, axis=0)
            p = jnp.exp(sv - m)
            s_scr[t, 0:N_KEYS + 1] = p
            m_t.append(m)
            l_t.append(jnp.sum(p, axis=0))

        def pv_body(i, accs, d=d, c_ref=c_ref):
            return tuple(accs[t] + s_scr[t, i] * c_ref[i + t // d, t % d, 1] for t in range(n_t))
        accs = lax.fori_loop(0, n_cache, pv_body,
                             tuple(jnp.zeros((HEADS_PER_GROUP, HEAD_DIM), F32) for _ in range(n_t)))
        accs = list(accs)
        for i in range(n_cache, N_KEYS + 1):
            for t in range(n_t):
                accs[t] = accs[t] + s_scr[t, i] * key_tile(1, t, i)
        outs.append([accs[t] / l_t[t] for t in range(n_t)])
        lses.append([m_t[t] + jnp.log(l_t[t]) for t in range(n_t)])

    for t in range(n_t):
        l0, l1, l2 = lses[0][t], lses[1][t], lses[2][t]
        mx = jnp.maximum(jnp.maximum(l0, l1), l2)
        e0, e1, e2 = jnp.exp(l0 - mx), jnp.exp(l1 - mx), jnp.exp(l2 - mx)
        merged = (e0 * outs[0][t] + e1 * outs[1][t] + e2 * outs[2][t]) / (e0 + e1 + e2)
        gate = gate_ref[t]
        u_ref[t] = gate * jax.nn.sigmoid(gate) * merged

    for g, o_ref in enumerate((o0_ref, o1_ref, o2_ref)):
        heads = slice(g * HEADS_PER_GROUP, (g + 1) * HEADS_PER_GROUP)
        o_ref[:, 0] = kn_ref[:, heads, :]
        o_ref[:, 1] = vn_ref[:, heads, :]


def _decode_attn(layer, proj, bias_dec, caches):
    bd, n_t = proj.shape[:2]
    assert n_t == SUBLANES
    views, view_specs = [], []
    for g, cache in enumerate(caches):
        d = DILATIONS[g]
        n_res = min(d, n_t)
        views.append(cache.reshape(cache.shape[0], bd, N_KEYS, d, 2, HEADS_PER_GROUP, HEAD_DIM))
        view_specs.append(pl.BlockSpec((None, None, N_KEYS, n_res, 2, HEADS_PER_GROUP, HEAD_DIM),
                                       lambda b: (layer, b, 0, 0, 0, 0, 0)))
    qkv_specs = [pl.BlockSpec((None, n_t, N_HEADS, HEAD_DIM), lambda b, i=i: (b, 0, i, 0)) for i in range(3)]
    gate_spec = pl.BlockSpec((None, n_t, HEADS_PER_GROUP, HEAD_DIM),
                             lambda b: (b, 0, 3 * N_HEADS // HEADS_PER_GROUP, 0))
    u_spec = pl.BlockSpec((None, n_t, HEADS_PER_GROUP, HEAD_DIM), lambda b: (b, 0, 0, 0))
    new_rows_shape = (bd, n_t, 2, HEADS_PER_GROUP, HEAD_DIM)
    new_rows_spec = pl.BlockSpec((None,) + new_rows_shape[1:], lambda b: (b, 0, 0, 0, 0))
    res = pl.pallas_call(
        _decode_kernel,
        out_shape=[jax.ShapeDtypeStruct((bd, n_t, HEADS_PER_GROUP, HEAD_DIM), F32)]
        + [jax.ShapeDtypeStruct(new_rows_shape, c.dtype) for c in caches],
        grid=(bd,),
        in_specs=qkv_specs + [gate_spec, pl.BlockSpec(bias_dec.shape, lambda b: (0, 0, 0, 0))] + view_specs,
        out_specs=[u_spec] + [new_rows_spec] * N_GROUPS,
        scratch_shapes=[pltpu.VMEM((n_t, N_KEYS + SUBLANES, HEADS_PER_GROUP, HEAD_DIM), F32)],
        compiler_params=_params(("parallel",)),
        name="decode_attn",
    )(proj, proj, proj, proj, bias_dec, *views)
    return res[0], tuple(res[1:])


def _kv_tail_kernel(p0_ref, p1_ref, o_ref):
    layer = pl.program_id(0)
    for p_ref, which in ((p0_ref, 0), (p1_ref, 1)):
        @pl.when(layer == which)
        def _(p_ref=p_ref):
            for h in range(HEADS_PER_GROUP):
                o_ref[:, h, :] = p_ref[:, h * HEAD_DIM:(h + 1) * HEAD_DIM]


def _kv_tails(projs, g, batch, seq):
    d = DILATIONS[g]
    keep = min(WINDOWS[g], seq)
    per_res = keep // d
    assert len(projs) == 2 and keep <= SPAN and keep % d == 0 and (SPAN // d) % per_res == 0
    assert seq % per_res == 0 and per_res % SUBLANES == 0

    def in_rows(layer, b, kv, r):
        first = (b + 1) * seq - SPAN + r * (SPAN // d) + (SPAN - keep) // d
        return first // per_res, (1 + kv) * (QKV_WIDTH // ATTN_WIDTH) + g

    out = pl.pallas_call(
        _kv_tail_kernel,
        out_shape=jax.ShapeDtypeStruct((len(projs), batch, per_res, d, 2, HEADS_PER_GROUP, HEAD_DIM), F32),
        grid=(len(projs), batch, 2, d),
        in_specs=[pl.BlockSpec((per_res, ATTN_WIDTH), in_rows)] * 2,
        out_specs=pl.BlockSpec((None, None, per_res, None, None, HEADS_PER_GROUP, HEAD_DIM),
                               lambda layer, b, kv, r: (layer, b, 0, r, kv, 0, 0)),
        compiler_params=_params(("parallel",) * 4),
        name="kv_tails",
    )(*projs)
    return out.reshape(len(projs), batch, keep, 2, HEADS_PER_GROUP, HEAD_DIM)


def _cache_update_kernel(cur_ref, next_ref, new_ref, o_ref):
    rows_, n_new = cur_ref.shape[0], new_ref.shape[0]
    o_ref[0:rows_ - n_new] = cur_ref[n_new:rows_]
    last = pl.program_id(1) == pl.num_programs(1) - 1

    @pl.when(jnp.logical_not(last))
    def _():
        o_ref[rows_ - n_new:rows_] = next_ref[...]

    @pl.when(last)
    def _():
        o_ref[rows_ - n_new:rows_] = new_ref[...]


def _cache_update(cache, new_rows):
    n_layers, batch, window = cache.shape[:3]
    n_new = new_rows.shape[2]
    chunk = min(window, CACHE_CHUNK_ROWS)
    assert window % chunk == 0 and chunk % n_new == 0
    tile = cache.shape[3:]
    zeros = (0,) * len(tile)
    next_blocks = chunk // n_new
    last_block = window // n_new - 1
    return pl.pallas_call(
        _cache_update_kernel,
        out_shape=jax.ShapeDtypeStruct(cache.shape, cache.dtype),
        grid=(n_layers * batch, window // chunk),
        in_specs=[
            pl.BlockSpec((None, None, chunk) + tile, lambda i, c: (i // batch, i % batch, c) + zeros),
            pl.BlockSpec((None, None, n_new) + tile,
                         lambda i, c: (i // batch, i % batch, jnp.minimum((c + 1) * next_blocks, last_block))
                         + zeros),
            pl.BlockSpec((None, None, n_new) + tile, lambda i, c: (i // batch, i % batch, 0) + zeros),
        ],
        out_specs=pl.BlockSpec((None, None, chunk) + tile, lambda i, c: (i // batch, i % batch, c) + zeros),
        compiler_params=_params(("parallel", "arbitrary")),
        name="cache_update",
    )(cache, cache, new_rows)


def _scan_rows(a, b, h_prev):
    n_tiles = a.shape[0] // SUBLANES
    a3 = a.reshape(n_tiles, SUBLANES, a.shape[1])
    b3 = b.reshape(n_tiles, SUBLANES, b.shape[1])
    row = lax.broadcasted_iota(jnp.int32, a3.shape, 1)
    shift = 1
    while shift < SUBLANES:
        a_sh = pltpu.roll(a3, shift, axis=1)
        b_sh = pltpu.roll(b3, shift, axis=1)
        keep = row >= shift
        b3 = jnp.where(keep, a3 * b_sh + b3, b3)
        a3 = jnp.where(keep, a3 * a_sh, a3)
        shift *= 2
    tiles = []
    carry = h_prev
    for k in range(n_tiles):
        h_k = b3[k] + a3[k] * carry
        carry = h_k[SUBLANES - 1:SUBLANES]
        tiles.append(h_k)
    return jnp.concatenate(tiles, axis=0), carry


def _rnn_kernel(x_ref, ng_ref, win_ref, conv0_ref, h0_ref, cw_ref, cb_ref, gaw_ref, gab_ref,
                gxw_ref, gxb_ref, lam_ref, wout_ref, y_ref, hlast_ref, clast_ref,
                xe_scr, gate_scr, h_scr, u_scr):
    c = pl.program_id(1)
    tl = x_ref.shape[0]
    pad = SUBLANES

    @pl.when(c == 0)
    def _():
        xe_scr[pad - (CONV_W - 1):pad] = conv0_ref[...]
        h_scr[...] = h0_ref[...]

    @pl.when(c > 0)
    def _():
        xe_scr[0:pad] = xe_scr[tl:tl + pad]

    x = x_ref[...]
    ms = jnp.mean(x * x, axis=-1, keepdims=True)
    hn = (x * lax.rsqrt(ms + EPS) * ng_ref[...]).astype(BF16)
    xe_scr[pad:pad + tl] = jnp.dot(hn, win_ref[:, :D_RNN], preferred_element_type=F32)
    gate_scr[...] = jnp.dot(hn, win_ref[:, D_RNN:], preferred_element_type=F32)
    clast_ref[...] = xe_scr[pad + tl - (CONV_W - 1):pad + tl]

    for j in range(RNN_BLOCKS):
        cols = slice(j * RNN_BLOCK_W, (j + 1) * RNN_BLOCK_W)
        xc = cb_ref[:, cols]
        for k in range(CONV_W):
            start = pad - (CONV_W - 1) + k
            xc = xc + cw_ref[k:k + 1, cols] * xe_scr[start:start + tl, cols]
        xcb = xc.astype(BF16)
        r = jax.nn.sigmoid(jnp.dot(xcb, gaw_ref[j], preferred_element_type=F32) + gab_ref[:, cols])
        ig = jax.nn.sigmoid(jnp.dot(xcb, gxw_ref[j], preferred_element_type=F32) + gxb_ref[:, cols])
        neg_lam = -lam_ref[:, cols]
        softplus = jnp.maximum(neg_lam, 0.0) + jnp.log1p(jnp.exp(-jnp.abs(neg_lam)))
        log_a = -LRU_C * r * softplus
        a = jnp.exp(log_a)
        bterm = jnp.sqrt(-jnp.tanh(log_a) * (a * a + 1.0)) * (ig * xc)
        h, h_end = _scan_rows(a, bterm, h_scr[:, cols])
        h_scr[:, cols] = h_end
        gate = gate_scr[:, cols]
        u_scr[:, cols] = gate * jax.nn.sigmoid(gate) * h

    hlast_ref[...] = h_scr[...]
    y_ref[...] = x_ref[...] + jnp.dot(u_scr[...].astype(BF16), wout_ref[...], preferred_element_type=F32)


def _rnn_layer(x, norm_g, w_in, conv0, h0, conv_w, conv_b, ga_w, ga_b, gx_w, gx_b, lam, w_out, batch, seq, tl):
    n_chunks = seq // tl
    assert seq % tl == 0 and tl % SUBLANES == 0
    row = lambda b, c: (b * n_chunks + c, 0)
    vec = lambda a: a.reshape(1, -1)
    full = lambda shape: pl.BlockSpec(shape, lambda b, c: (0,) * len(shape))
    per_batch = lambda rows_: pl.BlockSpec((None, rows_, D_RNN), lambda b, c: (b, 0, 0))
    y, h_last, c_last = pl.pallas_call(
        _rnn_kernel,
        out_shape=[jax.ShapeDtypeStruct((batch * seq, D_MODEL), F32),
                   jax.ShapeDtypeStruct((batch, 1, D_RNN), F32),
                   jax.ShapeDtypeStruct((batch, CONV_W - 1, D_RNN), F32)],
        grid=(batch, n_chunks),
        in_specs=[
            pl.BlockSpec((tl, D_MODEL), row),
            full((1, D_MODEL)),
            full((D_MODEL, 2 * D_RNN)),
            per_batch(CONV_W - 1),
            per_batch(1),
            full((CONV_W, D_RNN)),
            full((1, D_RNN)),
            full((RNN_BLOCKS, RNN_BLOCK_W, RNN_BLOCK_W)),
            full((1, D_RNN)),
            full((RNN_BLOCKS, RNN_BLOCK_W, RNN_BLOCK_W)),
            full((1, D_RNN)),
            full((1, D_RNN)),
            full((D_RNN, D_MODEL)),
        ],
        out_specs=[pl.BlockSpec((tl, D_MODEL), row), per_batch(1), per_batch(CONV_W - 1)],
        scratch_shapes=[
            pltpu.VMEM((tl + SUBLANES, D_RNN), F32),
            pltpu.VMEM((tl, D_RNN), F32),
            pltpu.VMEM((1, D_RNN), F32),
            pltpu.VMEM((tl, D_RNN), F32),
        ],
        compiler_params=_params(("parallel", "arbitrary")),
        name="rnn_layer",
    )(x, vec(norm_g), w_in, conv0, h0.reshape(batch, 1, D_RNN), conv_w, vec(conv_b), ga_w, vec(ga_b),
      gx_w, vec(gx_b), vec(lam), w_out)
    return y, h_last.reshape(batch, D_RNN), c_last


def _t5_bucket(dist):
    n = jnp.maximum(dist, 0)
    max_exact = N_BUCKETS // 2
    nf = jnp.maximum(n, 1).astype(F32)
    large = max_exact + (jnp.log(nf / max_exact) / math.log(MAX_DISTANCE / max_exact)
                         * (N_BUCKETS - max_exact)).astype(jnp.int32)
    large = jnp.minimum(large, N_BUCKETS - 1)
    return jnp.where(n < max_exact, n, large)


def _bias_tables(rel_bias):
    steps = jnp.arange(N_KEYS + 1)
    period = 3 * Q_BLK
    bands, decs = [], []
    for g in range(N_GROUPS):
        heads = slice(g * HEADS_PER_GROUP, (g + 1) * HEADS_PER_GROUP)
        per_step = rel_bias[_t5_bucket(steps * DILATIONS[g])][:, heads]
        row = jnp.concatenate([per_step[::-1].T,
                               jnp.full((HEADS_PER_GROUP, period - (N_KEYS + 1)), NEG, F32)], axis=1)
        band = jnp.tile(row, (1, Q_BLK))[:, :Q_BLK * (period - 1)]
        bands.append(band.reshape(HEADS_PER_GROUP, Q_BLK, period - 1)[:, :, :2 * Q_BLK])
        decs.append(jnp.broadcast_to(per_step[::-1][:, :, None],
                                     (N_KEYS + 1, HEADS_PER_GROUP, HEAD_DIM)))
    return jnp.stack(bands, axis=1).astype(F32), jnp.stack(decs).astype(F32)


def kernel(x_prompt, x_sample, cache_kv_w128, cache_kv_w512, cache_kv_w2048, state_rglru_h,
           state_rglru_conv, attn_norm, attn_w_in, attn_q_norm, attn_k_norm, attn_w_out, rel_bias,
           rnn_norm, rnn_w_in, rnn_conv_w, rnn_conv_b, rnn_gate_a_w, rnn_gate_a_b, rnn_gate_x_w,
           rnn_gate_x_b, rnn_lambda, rnn_w_out):
    batch, seq, _ = x_prompt.shape
    bd, n_t, _ = x_sample.shape
    depth = attn_norm.shape[0] + rnn_norm.shape[0]
    caches = (cache_kv_w128, cache_kv_w512, cache_kv_w2048)

    yp = x_prompt.reshape(batch * seq, D_MODEL)
    ys = x_sample.reshape(bd * n_t, D_MODEL)
    tm_p, tm_s = 1024, bd * n_t

    bias_band, bias_dec = _bias_tables(rel_bias)
    projs_p = []
    kv_new = [[] for _ in range(N_GROUPS)]
    h_p, h_s, c_p, c_s = [], [], [], []
    for i in range(depth):
        li = i // 2
        if i % 2 == 0:
            w_in = attn_w_in[li].astype(BF16)
            w_out = attn_w_out[li].astype(BF16)
            head_scale = jnp.concatenate([
                jnp.tile(attn_q_norm[li] * (HEAD_DIM ** -0.5), N_HEADS),
                jnp.tile(attn_k_norm[li], N_HEADS)]).reshape(1, 2 * QKV_WIDTH)
            proj_p = _norm_proj(yp, attn_norm[li], w_in, head_scale, SPAN, 1024, row_orders=DILATIONS)
            proj_s = _norm_proj(ys, attn_norm[li], w_in, head_scale, tm_s, 1024, split_heads=True)
            projs_p.append(proj_p)

            u_p = _attn_prompt(proj_p, bias_band, batch, seq)
            yp = _out_proj(u_p, w_out, yp, tm_p)

            u_s, new_rows = _decode_attn(li, proj_s.reshape(bd, n_t, -1, HEAD_DIM), bias_dec, caches)
            for g in range(N_GROUPS):
                kv_new[g].append(new_rows[g])
            ys = _out_proj(u_s.reshape(bd * n_t, ATTN_WIDTH), w_out, ys, tm_s)
        else:
            w_in = rnn_w_in[li].astype(BF16)
            rnn_args = (rnn_conv_w[li], rnn_conv_b[li], rnn_gate_a_w[li].astype(BF16), rnn_gate_a_b[li].reshape(-1),
                        rnn_gate_x_w[li].astype(BF16), rnn_gate_x_b[li].reshape(-1), rnn_lambda[li],
                        rnn_w_out[li].astype(BF16))
            yp, hp, cp = _rnn_layer(yp, rnn_norm[li], w_in, jnp.zeros((batch, CONV_W - 1, D_RNN), F32),
                                    jnp.zeros((batch, D_RNN), F32), *rnn_args, batch, seq, RNN_CHUNK)
            ys, hs, cs = _rnn_layer(ys, rnn_norm[li], w_in, state_rglru_conv[li], state_rglru_h[li], *rnn_args,
                                    bd, n_t, n_t)
            h_p.append(hp)
            h_s.append(hs)
            c_p.append(cp)
            c_s.append(cs)

    shifted = [_cache_update(caches[g], jnp.stack(kv_new[g])) for g in range(N_GROUPS)]
    kv_p = [_kv_tails(projs_p, g, batch, seq) for g in range(N_GROUPS)]
    return (yp.reshape(batch, seq, D_MODEL), ys.reshape(bd, n_t, D_MODEL),
            kv_p[0], shifted[0],
            kv_p[1], shifted[1],
            kv_p[2], shifted[2],
            jnp.stack(h_p), jnp.stack(h_s),
            jnp.stack(c_p), jnp.stack(c_s))
```

```python
import functools
import math

import jax
import jax.numpy as jnp
from jax import lax
from jax.experimental import pallas as pl
from jax.experimental.pallas import tpu as pltpu

F32 = jnp.float32
BF16 = jnp.bfloat16

D_MODEL = 1024
N_GROUPS = 3
WINDOWS = (128, 512, 2048)
DILATIONS = (1, 4, 16)
HEADS_PER_GROUP = 8
HEAD_DIM = 128
N_HEADS = N_GROUPS * HEADS_PER_GROUP
QKV_WIDTH = N_HEADS * HEAD_DIM
ATTN_WIDTH = HEADS_PER_GROUP * HEAD_DIM
Q_BLK = 128
N_KEYS = Q_BLK
SPAN = Q_BLK * DILATIONS[-1]
RNN_CHUNK = 256
COPY_CHUNK_ROWS = 256
COPY_SLOTS = 6
SCORE_KEYS = 8
SCORE_UNROLL = 4
N_BUCKETS = 32
MAX_DISTANCE = 2048
D_RNN = 1280
RNN_BLOCKS = 10
RNN_BLOCK_W = D_RNN // RNN_BLOCKS
CONV_W = 4
LRU_C = 8.0
EPS = 1e-6
NEG = -1e30

SUBLANES = 8
MXU_COLS = 256
VMEM_LIMIT = 52 * 1024 * 1024


def _params(semantics):
    return pltpu.CompilerParams(dimension_semantics=semantics, vmem_limit_bytes=VMEM_LIMIT)


def _residue_rows(start, size, stride):
    return pl.ds(start, size) if stride == 1 else pl.ds(start, size, stride=stride)


def _proj_kernel(*refs, n_norm_tiles, row_orders, tiles_per_order, n_slabs):
    x_refs, (g_ref, w_ref), rest = refs[:n_slabs], refs[n_slabs:n_slabs + 2], refs[n_slabs + 2:]
    hs_ref, o_ref, h_scr = rest
    n = pl.program_id(1)
    tm = x_refs[0].shape[0]
    k_dim = n_slabs * HEAD_DIM

    @pl.when(n == 0)
    def _():
        for i, d in enumerate(row_orders):
            for r in range(d):
                xs = [x_ref[_residue_rows(r, tm // d, d), :] for x_ref in x_refs]
                ms = jnp.sum(sum(x * x for x in xs), axis=-1, keepdims=True) * (1.0 / k_dim)
                inv = lax.rsqrt(ms + EPS)
                for c, x in enumerate(xs):
                    cols = slice(c * HEAD_DIM, (c + 1) * HEAD_DIM)
                    h_scr[i, r * (tm // d):(r + 1) * (tm // d), cols] = (x * inv * g_ref[:, cols]).astype(BF16)

    if len(row_orders) == 1:
        h = h_scr[0]
    else:
        order = jnp.where(n < tiles_per_order * len(row_orders) * 3, (n // tiles_per_order) % len(row_orders), 0)
        h = h_scr[order]
    tn = w_ref.shape[1]

    def store(head, value):
        if len(o_ref.shape) == 3:
            o_ref[:, head, :] = value
        else:
            o_ref[:, head * HEAD_DIM:(head + 1) * HEAD_DIM] = value

    @pl.when(n < n_norm_tiles)
    def _():
        for c in range(tn // MXU_COLS):
            acc = jnp.dot(h, w_ref[:, c * MXU_COLS:(c + 1) * MXU_COLS], preferred_element_type=F32)
            for hh in range(MXU_COLS // HEAD_DIM):
                head = c * (MXU_COLS // HEAD_DIM) + hh
                a = acc[:, hh * HEAD_DIM:(hh + 1) * HEAD_DIM]
                ms = jnp.mean(a * a, axis=-1, keepdims=True)
                store(head, a * lax.rsqrt(ms + EPS) * hs_ref[:, head * HEAD_DIM:(head + 1) * HEAD_DIM])

    @pl.when(n >= n_norm_tiles)
    def _():
        acc = jnp.dot(h, w_ref[...], preferred_element_type=F32)
        if len(o_ref.shape) == 3:
            for head in range(tn // HEAD_DIM):
                store(head, acc[:, head * HEAD_DIM:(head + 1) * HEAD_DIM])
        else:
            o_ref[...] = acc


def _norm_proj(x, norm_g, w, head_scale, tm, tn, row_orders=(1,), split_heads=False):
    m_rows, k_dim = x.shape
    n_cols = w.shape[1]
    n_norm_cols = head_scale.shape[1]
    assert m_rows % tm == 0 and n_cols % tn == 0 and n_norm_cols % tn == 0 and tn % MXU_COLS == 0
    assert len(row_orders) == 1 or ATTN_WIDTH % tn == 0
    n_slabs = k_dim // HEAD_DIM
    last_norm_tile = n_norm_cols // tn - 1
    if split_heads:
        out_shape = (m_rows, n_cols // HEAD_DIM, HEAD_DIM)
        out_spec = pl.BlockSpec((tm, tn // HEAD_DIM, HEAD_DIM), lambda m, n: (m, n, 0))
    else:
        out_shape = (m_rows, n_cols)
        out_spec = pl.BlockSpec((tm, tn), lambda m, n: (m, n))
    in_specs = [pl.BlockSpec((tm, HEAD_DIM), lambda m, n, c=c: (m, c), pipeline_mode=pl.Buffered(1))
                for c in range(n_slabs)] + [
        pl.BlockSpec((1, k_dim), lambda m, n: (0, 0)),
        pl.BlockSpec((k_dim, tn), lambda m, n: (0, n)),
        pl.BlockSpec((1, tn), lambda m, n: (0, jnp.minimum(n, last_norm_tile))),
    ]
    args = [x] * n_slabs + [norm_g.reshape(1, k_dim), w, head_scale]
    return pl.pallas_call(
        functools.partial(_proj_kernel, n_norm_tiles=n_norm_cols // tn, row_orders=row_orders,
                          tiles_per_order=ATTN_WIDTH // tn, n_slabs=n_slabs),
        out_shape=jax.ShapeDtypeStruct(out_shape, F32),
        grid=(m_rows // tm, n_cols // tn),
        in_specs=in_specs,
        out_specs=out_spec,
        scratch_shapes=[pltpu.VMEM((len(row_orders), tm, k_dim), BF16)],
        compiler_params=_params(("parallel", "arbitrary")),
        name="norm_proj",
    )(*args)


def _out_kernel(u_ref, w_ref, x_ref, y_ref):
    y_ref[...] = x_ref[...] + jnp.dot(u_ref[...].astype(BF16), w_ref[...], preferred_element_type=F32)


def _out_proj(u, w, x, tm):
    m_rows, k_dim = u.shape
    n_cols = w.shape[1]
    assert m_rows % tm == 0
    return pl.pallas_call(
        _out_kernel,
        out_shape=jax.ShapeDtypeStruct((m_rows, n_cols), F32),
        grid=(m_rows // tm,),
        in_specs=[
            pl.BlockSpec((tm, k_dim), lambda m: (m, 0)),
            pl.BlockSpec((k_dim, n_cols), lambda m: (0, 0)),
            pl.BlockSpec((tm, n_cols), lambda m: (m, 0)),
        ],
        out_specs=pl.BlockSpec((tm, n_cols), lambda m: (m, 0)),
        compiler_params=_params(("parallel",)),
        name="out_proj",
    )(u, w, x)


def _attn_prompt_kernel(q0, q1, q2, k0, k1, k2, v0, v1, v2, gate_ref, bias_ref, u_ref,
                        kx, vx, o_scr, l_scr):
    span = pl.program_id(2)
    q_refs = (q0, q1, q2)
    n_blk = SPAN // Q_BLK
    cur, prev = span % 2, 1 - span % 2

    @pl.when(span == 0)
    def _():
        kx[:, 1] = jnp.zeros((N_GROUPS, SPAN, HEAD_DIM), BF16)
        vx[:, 1] = jnp.zeros((N_GROUPS, SPAN, HEAD_DIM), BF16)

    for g, (k_ref, v_ref) in enumerate(((k0, v0), (k1, v1), (k2, v2))):
        kx[g, cur] = k_ref[...].astype(BF16)
        vx[g, cur] = v_ref[...].astype(BF16)

    first_span = span == 0

    def blk_rows(bi):
        return slice(bi * Q_BLK, (bi + 1) * Q_BLK)

    def prev_block(g, bi):
        per_residue = n_blk // DILATIONS[g]
        if bi % per_residue:
            return cur, bi - 1, None
        return prev, bi + per_residue - 1, first_span

    def token_rows(g, bi):
        d = DILATIONS[g]
        per_residue = n_blk // d
        return _residue_rows((bi % per_residue) * Q_BLK * d + bi // per_residue, Q_BLK, d)

    for g in range(N_GROUPS):
        each = range(n_blk)
        prevs = [prev_block(g, bi) for bi in each]
        q = [q_refs[g][blk_rows(bi), :].astype(BF16) for bi in each]
        k = [jnp.concatenate([kx[g, prevs[bi][0], blk_rows(prevs[bi][1]), :], kx[g, cur, blk_rows(bi), :]], axis=0)
             for bi in each]
        v = [jnp.concatenate([vx[g, prevs[bi][0], blk_rows(prevs[bi][1]), :], vx[g, cur, blk_rows(bi), :]], axis=0)
             for bi in each]
        s = [lax.dot_general(q[bi], k[bi], (((1,), (1,)), ((), ())), preferred_element_type=F32) + bias_ref[g]
             for bi in each]
        in_prev = lax.broadcasted_iota(jnp.int32, s[0].shape, 1) < Q_BLK
        s = [s[bi] if prevs[bi][2] is None else jnp.where(jnp.logical_and(prevs[bi][2], in_prev), NEG, s[bi])
             for bi in each]
        m = [jnp.max(s[bi], axis=-1, keepdims=True) for bi in each]
        p = [jnp.exp(s[bi] - m[bi]) for bi in each]
        l = [jnp.sum(p[bi], axis=-1, keepdims=True) for bi in each]
        o = [jnp.dot(p[bi].astype(BF16), v[bi], preferred_element_type=F32) / l[bi] for bi in each]
        for bi in each:
            o_scr[g, token_rows(g, bi), :] = o[bi]
            l_scr[g, token_rows(g, bi), :] = jnp.broadcast_to(m[bi] + jnp.log(l[bi]), (Q_BLK, HEAD_DIM))

    for c in range(n_blk):
        rows_ = blk_rows(c)
        l0, l1, l2 = l_scr[0, rows_, :], l_scr[1, rows_, :], l_scr[2, rows_, :]
        mx = jnp.maximum(jnp.maximum(l0, l1), l2)
        e0, e1, e2 = jnp.exp(l0 - mx), jnp.exp(l1 - mx), jnp.exp(l2 - mx)
        merged = (e0 * o_scr[0, rows_, :] + e1 * o_scr[1, rows_, :] + e2 * o_scr[2, rows_, :]) / (e0 + e1 + e2)
        gate = gate_ref[rows_, :]
        u_ref[rows_, :] = gate * jax.nn.sigmoid(gate) * merged


def _attn_prompt(proj, bias_band, batch, seq):
    n_span = seq // SPAN
    assert seq % SPAN == 0

    def col_spec(first_col_block):
        return pl.BlockSpec((SPAN, HEAD_DIM),
                            lambda b, h, s, c=first_col_block: (b * n_span + s, c + h))

    head_blocks = QKV_WIDTH // HEAD_DIM
    in_specs = (
        [col_spec(g * HEADS_PER_GROUP) for g in range(N_GROUPS)]
        + [col_spec(head_blocks + g * HEADS_PER_GROUP) for g in range(N_GROUPS)]
        + [col_spec(2 * head_blocks + g * HEADS_PER_GROUP) for g in range(N_GROUPS)]
        + [col_spec(3 * head_blocks)]
        + [pl.BlockSpec((None, N_GROUPS, Q_BLK, 2 * Q_BLK), lambda b, h, s: (h, 0, 0, 0))]
    )
    return pl.pallas_call(
        _attn_prompt_kernel,
        out_shape=jax.ShapeDtypeStruct((batch * seq, ATTN_WIDTH), F32),
        grid=(batch, HEADS_PER_GROUP, n_span),
        in_specs=in_specs,
        out_specs=pl.BlockSpec((SPAN, HEAD_DIM), lambda b, h, s: (b * n_span + s, h)),
        scratch_shapes=[
            pltpu.VMEM((N_GROUPS, 2, SPAN, HEAD_DIM), BF16),
            pltpu.VMEM((N_GROUPS, 2, SPAN, HEAD_DIM), BF16),
            pltpu.VMEM((N_GROUPS, SPAN, HEAD_DIM), F32),
            pltpu.VMEM((N_GROUPS, SPAN, HEAD_DIM), F32),
        ],
        compiler_params=_params(("parallel", "parallel", "arbitrary")),
        name="attn_prompt",
    )(*([proj] * 10), bias_band)


def _decode_kernel(*refs, layer, hosted, stale_other):
    n_h = len(hosted)
    (q_ref, kn_ref, vn_ref, gate_ref, bias_ref, c0_ref, c1_ref, c2_ref), refs = refs[:8], refs[8:]
    cache_hbm, refs = refs[:n_h], refs[n_h:]
    other_new, refs = (refs[:0], refs) if stale_other else (refs[:n_h], refs[n_h:])
    (u_ref, o0_ref, o1_ref, o2_ref), refs = refs[:4], refs[4:]
    out_hbm, (s_scr, buf, rows_scr, in_sems, out_sems, row_sems) = refs[:n_h], refs[n_h:]
    cache_refs = (c0_ref, c1_ref, c2_ref)
    new_refs = (o0_ref, o1_ref, o2_ref)
    n_t = q_ref.shape[0]
    b = pl.program_id(0)

    for g, o_ref in enumerate(new_refs):
        heads = slice(g * HEADS_PER_GROUP, (g + 1) * HEADS_PER_GROUP)
        o_ref[:, 0] = kn_ref[:, heads, :]
        o_ref[:, 1] = vn_ref[:, heads, :]

    n_slots, chunk_rows = buf.shape[:2]
    chunks = []
    for l in range(2):
        for hi, g in enumerate(hosted):
            kept = WINDOWS[g] - n_t
            spans = [(n_t + r0, r0, min(chunk_rows, kept - r0)) for r0 in range(0, kept, chunk_rows)]
            if stale_other and l != layer:
                spans.append((kept, kept, n_t))
            for src0, dst0, n in spans:
                slot = len(chunks) % n_slots
                chunks.append((
                    pltpu.make_async_copy(cache_hbm[hi].at[l, b, pl.ds(src0, n)], buf.at[slot, pl.ds(0, n)],
                                          in_sems.at[slot]),
                    pltpu.make_async_copy(buf.at[slot, pl.ds(0, n)], out_hbm[hi].at[l, b, pl.ds(dst0, n)],
                                          out_sems.at[slot])))
    row_copies = []
    for hi, g in enumerate(hosted):
        newest = pl.ds(WINDOWS[g] - n_t, n_t)
        rows_scr[2 * hi] = new_refs[g][...]
        row_copies.append(pltpu.make_async_copy(rows_scr.at[2 * hi], out_hbm[hi].at[layer, b, newest],
                                                row_sems.at[2 * hi]))
        if not stale_other:
            rows_scr[2 * hi + 1] = other_new[hi][...]
            row_copies.append(pltpu.make_async_copy(rows_scr.at[2 * hi + 1], out_hbm[hi].at[1 - layer, b, newest],
                                                    row_sems.at[2 * hi + 1]))
    for cp_in, _ in chunks[:n_slots]:
        cp_in.start()
    for cp in row_copies:
        cp.start()
    pumped = [0]

    def pump():
        i = pumped[0]
        if i == len(chunks):
            return
        chunks[i][0].wait()
        chunks[i][1].start()
        if i:
            chunks[i - 1][1].wait()
            if i - 1 + n_slots < len(chunks):
                chunks[i - 1 + n_slots][0].start()
        pumped[0] = i + 1

    outs, lses = [], []
    for g in range(N_GROUPS):
        d = DILATIONS[g]
        heads = slice(g * HEADS_PER_GROUP, (g + 1) * HEADS_PER_GROUP)
        c_ref = cache_refs[g]
        n_cache = N_KEYS - (n_t - 1) // d
        q_t = [q_ref[t, heads, :] for t in range(n_t)]

        def key_tile(kv, t, i, g=g, d=d, c_ref=c_ref, heads=heads):
            row = t + i * d
            if row < WINDOWS[g]:
                return c_ref[row // d, row % d, kv]
            return (kn_ref, vn_ref)[kv][row - WINDOWS[g], heads, :]

        def store_scores(keys, tiles, g=g):
            prod = jnp.concatenate([q_t[t] * tiles[ii][t] for ii in range(len(keys)) for t in range(n_t)], axis=0)
            sums = jnp.dot(prod.astype(BF16), jnp.ones((HEAD_DIM, HEAD_DIM), BF16), preferred_element_type=F32)
            for ii, i in enumerate(keys):
                for t in range(n_t):
                    row0 = (ii * n_t + t) * HEADS_PER_GROUP
                    s_scr[t, i] = sums[row0:row0 + HEADS_PER_GROUP] + bias_ref[g, i]

        def scores_body(it, carry, d=d, c_ref=c_ref, store_scores=store_scores):
            keys = [it * SCORE_KEYS + ii for ii in range(SCORE_KEYS)]
            store_scores(keys, [[c_ref[i + t // d, t % d, 0] for t in range(n_t)] for i in keys])
            return carry
        lax.fori_loop(0, n_cache // SCORE_KEYS, scores_body, 0, unroll=SCORE_UNROLL)
        pump()
        rest = list(range(n_cache // SCORE_KEYS * SCORE_KEYS, N_KEYS + 1))
        store_scores(rest, [[key_tile(0, t, i) for t in range(n_t)] for i in rest])
        pump()

        m_t, l_t = [], []
        for t in range(n_t):
            sv = s_scr[t, 0:N_KEYS + 1]
            m = jnp.max(sv, axis=0)
            p = jnp.exp(sv - m)
            s_scr[t, 0:N_KEYS + 1] = p
            m_t.append(m)
            l_t.append(jnp.sum(p, axis=0))
            if t % 2:
                pump()

        def pv_body(i, accs, d=d, c_ref=c_ref):
            return tuple(accs[t] + s_scr[t, i] * c_ref[i + t // d, t % d, 1] for t in range(n_t))
        accs = lax.fori_loop(0, n_cache, pv_body,
                             tuple(jnp.zeros((HEADS_PER_GROUP, HEAD_DIM), F32) for _ in range(n_t)))
        pump()
        accs = list(accs)
        for i in range(n_cache, N_KEYS + 1):
            for t in range(n_t):
                accs[t] = accs[t] + s_scr[t, i] * key_tile(1, t, i)
        outs.append([accs[t] / l_t[t] for t in range(n_t)])
        lses.append([m_t[t] + jnp.log(l_t[t]) for t in range(n_t)])

    for t in range(n_t):
        l0, l1, l2 = lses[0][t], lses[1][t], lses[2][t]
        mx = jnp.maximum(jnp.maximum(l0, l1), l2)
        e0, e1, e2 = jnp.exp(l0 - mx), jnp.exp(l1 - mx), jnp.exp(l2 - mx)
        merged = (e0 * outs[0][t] + e1 * outs[1][t] + e2 * outs[2][t]) / (e0 + e1 + e2)
        gate = gate_ref[t]
        u_ref[t] = gate * jax.nn.sigmoid(gate) * merged

    while pumped[0] < len(chunks):
        pump()
    chunks[-1][1].wait()
    for cp in row_copies:
        cp.wait()


def _decode_attn(layer, proj, bias_dec, caches, hosted, other_new=None):
    bd, n_t = proj.shape[:2]
    assert n_t == SUBLANES and all(c.shape[0] == 2 for c in caches)
    stale_other = other_new is None
    views, view_specs = [], []
    for g, cache in enumerate(caches):
        d = DILATIONS[g]
        n_res = min(d, n_t)
        views.append(cache.reshape(cache.shape[0], bd, N_KEYS, d, 2, HEADS_PER_GROUP, HEAD_DIM))
        view_specs.append(pl.BlockSpec((None, None, N_KEYS, n_res, 2, HEADS_PER_GROUP, HEAD_DIM),
                                       lambda b: (layer, b, 0, 0, 0, 0, 0)))
    qkv_specs = [pl.BlockSpec((None, n_t, N_HEADS, HEAD_DIM), lambda b, i=i: (b, 0, i, 0)) for i in range(3)]
    gate_spec = pl.BlockSpec((None, n_t, HEADS_PER_GROUP, HEAD_DIM),
                             lambda b: (b, 0, 3 * N_HEADS // HEADS_PER_GROUP, 0))
    u_spec = pl.BlockSpec((None, n_t, HEADS_PER_GROUP, HEAD_DIM), lambda b: (b, 0, 0, 0))
    new_rows_shape = (bd, n_t, 2, HEADS_PER_GROUP, HEAD_DIM)
    new_rows_spec = pl.BlockSpec((None,) + new_rows_shape[1:], lambda b: (b, 0, 0, 0, 0))
    any_spec = pl.BlockSpec(memory_space=pl.ANY)
    hosted_caches = [caches[g] for g in hosted]
    others = [] if stale_other else [other_new[g] for g in hosted]
    res = pl.pallas_call(
        functools.partial(_decode_kernel, layer=layer, hosted=tuple(hosted), stale_other=stale_other),
        out_shape=[jax.ShapeDtypeStruct((bd, n_t, HEADS_PER_GROUP, HEAD_DIM), F32)]
        + [jax.ShapeDtypeStruct(new_rows_shape, c.dtype) for c in caches]
        + [jax.ShapeDtypeStruct(c.shape, c.dtype) for c in hosted_caches],
        grid=(bd,),
        in_specs=qkv_specs + [gate_spec, pl.BlockSpec(bias_dec.shape, lambda b: (0, 0, 0, 0))] + view_specs
        + [any_spec] * len(hosted) + [new_rows_spec] * len(others),
        out_specs=[u_spec] + [new_rows_spec] * N_GROUPS + [any_spec] * len(hosted),
        scratch_shapes=[
            pltpu.VMEM((n_t, N_KEYS + SUBLANES, HEADS_PER_GROUP, HEAD_DIM), F32),
            pltpu.VMEM((COPY_SLOTS, COPY_CHUNK_ROWS, 2, HEADS_PER_GROUP, HEAD_DIM), F32),
            pltpu.VMEM((2 * len(hosted), n_t, 2, HEADS_PER_GROUP, HEAD_DIM), F32),
            pltpu.SemaphoreType.DMA((COPY_SLOTS,)),
            pltpu.SemaphoreType.DMA((COPY_SLOTS,)),
            pltpu.SemaphoreType.DMA((2 * len(hosted),)),
        ],
        compiler_params=_params(("parallel",)),
        name="decode_attn",
    )(proj, proj, proj, proj, bias_dec, *views, *hosted_caches, *others)
    return res[0], tuple(res[1:1 + N_GROUPS]), tuple(res[1 + N_GROUPS:])


def _kv_tail_kernel(p0_ref, p1_ref, o_ref):
    layer = pl.program_id(0)
    for p_ref, which in ((p0_ref, 0), (p1_ref, 1)):
        @pl.when(layer == which)
        def _(p_ref=p_ref):
            for h in range(HEADS_PER_GROUP):
                o_ref[:, h, :] = p_ref[:, h * HEAD_DIM:(h + 1) * HEAD_DIM]


def _kv_tails(projs, g, batch, seq):
    d = DILATIONS[g]
    keep = min(WINDOWS[g], seq)
    per_res = keep // d
    assert len(projs) == 2 and keep <= SPAN and keep % d == 0 and (SPAN // d) % per_res == 0
    assert seq % per_res == 0 and per_res % SUBLANES == 0

    def in_rows(layer, b, kv, r):
        first = (b + 1) * seq - SPAN + r * (SPAN // d) + (SPAN - keep) // d
        return first // per_res, (1 + kv) * (QKV_WIDTH // ATTN_WIDTH) + g

    out = pl.pallas_call(
        _kv_tail_kernel,
        out_shape=jax.ShapeDtypeStruct((len(projs), batch, per_res, d, 2, HEADS_PER_GROUP, HEAD_DIM), F32),
        grid=(len(projs), batch, 2, d),
        in_specs=[pl.BlockSpec((per_res, ATTN_WIDTH), in_rows)] * 2,
        out_specs=pl.BlockSpec((None, None, per_res, None, None, HEADS_PER_GROUP, HEAD_DIM),
                               lambda layer, b, kv, r: (layer, b, 0, r, kv, 0, 0)),
        compiler_params=_params(("parallel",) * 4),
        name="kv_tails",
    )(*projs)
    return out.reshape(len(projs), batch, keep, 2, HEADS_PER_GROUP, HEAD_DIM)


def _set_rows_kernel(new_ref, cache_ref, o_ref):
    del cache_ref
    o_ref[...] = new_ref[...]


def _cache_set_rows(cache, new_rows, layer):
    batch, n_new = new_rows.shape[:2]
    tile = new_rows.shape[2:]
    zeros = (0,) * len(tile)
    last_block = cache.shape[2] // n_new - 1
    assert cache.shape[2] % n_new == 0
    return pl.pallas_call(
        _set_rows_kernel,
        out_shape=jax.ShapeDtypeStruct(cache.shape, cache.dtype),
        grid=(batch,),
        in_specs=[pl.BlockSpec((None, n_new) + tile, lambda b: (b, 0) + zeros),
                  pl.BlockSpec(memory_space=pl.ANY)],
        out_specs=pl.BlockSpec((None, None, n_new) + tile, lambda b: (layer, b, last_block) + zeros),
        input_output_aliases={1: 0},
        compiler_params=_params(("parallel",)),
        name="cache_set_rows",
    )(new_rows, cache)


def _scan_rows(a, b, h_prev):
    n_tiles = a.shape[0] // SUBLANES
    a3 = a.reshape(n_tiles, SUBLANES, a.shape[1])
    b3 = b.reshape(n_tiles, SUBLANES, b.shape[1])
    row = lax.broadcasted_iota(jnp.int32, a3.shape, 1)
    shift = 1
    while shift < SUBLANES:
        a_sh = pltpu.roll(a3, shift, axis=1)
        b_sh = pltpu.roll(b3, shift, axis=1)
        keep = row >= shift
        b3 = jnp.where(keep, a3 * b_sh + b3, b3)
        a3 = jnp.where(keep, a3 * a_sh, a3)
        shift *= 2
    tiles = []
    carry = h_prev
    for k in range(n_tiles):
        h_k = b3[k] + a3[k] * carry
        carry = h_k[SUBLANES - 1:SUBLANES]
        tiles.append(h_k)
    return jnp.concatenate(tiles, axis=0), carry


def _rnn_kernel(x_ref, ng_ref, win_ref, conv0_ref, h0_ref, cw_ref, cb_ref, gaw_ref, gab_ref,
                gxw_ref, gxb_ref, lam_ref, wout_ref, y_ref, hlast_ref, clast_ref,
                xe_scr, gate_scr, h_scr, u_scr):
    c = pl.program_id(1)
    tl = x_ref.shape[0]
    pad = SUBLANES

    @pl.when(c == 0)
    def _():
        xe_scr[pad - (CONV_W - 1):pad] = conv0_ref[...]
        h_scr[...] = h0_ref[...]

    @pl.when(c > 0)
    def _():
        xe_scr[0:pad] = xe_scr[tl:tl + pad]

    x = x_ref[...]
    ms = jnp.mean(x * x, axis=-1, keepdims=True)
    hn = (x * lax.rsqrt(ms + EPS) * ng_ref[...]).astype(BF16)
    xe_scr[pad:pad + tl] = jnp.dot(hn, win_ref[:, :D_RNN], preferred_element_type=F32)
    gate_scr[...] = jnp.dot(hn, win_ref[:, D_RNN:], preferred_element_type=F32)
    clast_ref[...] = xe_scr[pad + tl - (CONV_W - 1):pad + tl]

    for j in range(RNN_BLOCKS):
        cols = slice(j * RNN_BLOCK_W, (j + 1) * RNN_BLOCK_W)
        xc = cb_ref[:, cols]
        for k in range(CONV_W):
            start = pad - (CONV_W - 1) + k
            xc = xc + cw_ref[k:k + 1, cols] * xe_scr[start:start + tl, cols]
        xcb = xc.astype(BF16)
        r = jax.nn.sigmoid(jnp.dot(xcb, gaw_ref[j], preferred_element_type=F32) + gab_ref[:, cols])
        ig = jax.nn.sigmoid(jnp.dot(xcb, gxw_ref[j], preferred_element_type=F32) + gxb_ref[:, cols])
        neg_lam = -lam_ref[:, cols]
        softplus = jnp.maximum(neg_lam, 0.0) + jnp.log1p(jnp.exp(-jnp.abs(neg_lam)))
        log_a = -LRU_C * r * softplus
        a = jnp.exp(log_a)
        bterm = jnp.sqrt(-jnp.tanh(log_a) * (a * a + 1.0)) * (ig * xc)
        h, h_end = _scan_rows(a, bterm, h_scr[:, cols])
        h_scr[:, cols] = h_end
        gate = gate_scr[:, cols]
        u_scr[:, cols] = gate * jax.nn.sigmoid(gate) * h

    hlast_ref[...] = h_scr[...]
    y_ref[...] = x_ref[...] + jnp.dot(u_scr[...].astype(BF16), wout_ref[...], preferred_element_type=F32)


def _rnn_layer(x, norm_g, w_in, conv0, h0, conv_w, conv_b, ga_w, ga_b, gx_w, gx_b, lam, w_out, batch, seq, tl):
    n_chunks = seq // tl
    assert seq % tl == 0 and tl % SUBLANES == 0
    row = lambda b, c: (b * n_chunks + c, 0)
    vec = lambda a: a.reshape(1, -1)
    full = lambda shape: pl.BlockSpec(shape, lambda b, c: (0,) * len(shape))
    per_batch = lambda rows_: pl.BlockSpec((None, rows_, D_RNN), lambda b, c: (b, 0, 0))
    y, h_last, c_last = pl.pallas_call(
        _rnn_kernel,
        out_shape=[jax.ShapeDtypeStruct((batch * seq, D_MODEL), F32),
                   jax.ShapeDtypeStruct((batch, 1, D_RNN), F32),
                   jax.ShapeDtypeStruct((batch, CONV_W - 1, D_RNN), F32)],
        grid=(batch, n_chunks),
        in_specs=[
            pl.BlockSpec((tl, D_MODEL), row),
            full((1, D_MODEL)),
            full((D_MODEL, 2 * D_RNN)),
            per_batch(CONV_W - 1),
            per_batch(1),
            full((CONV_W, D_RNN)),
            full((1, D_RNN)),
            full((RNN_BLOCKS, RNN_BLOCK_W, RNN_BLOCK_W)),
            full((1, D_RNN)),
            full((RNN_BLOCKS, RNN_BLOCK_W, RNN_BLOCK_W)),
            full((1, D_RNN)),
            full((1, D_RNN)),
            full((D_RNN, D_MODEL)),
        ],
        out_specs=[pl.BlockSpec((tl, D_MODEL), row), per_batch(1), per_batch(CONV_W - 1)],
        scratch_shapes=[
            pltpu.VMEM((tl + SUBLANES, D_RNN), F32),
            pltpu.VMEM((tl, D_RNN), F32),
            pltpu.VMEM((1, D_RNN), F32),
            pltpu.VMEM((tl, D_RNN), F32),
        ],
        compiler_params=_params(("parallel", "arbitrary")),
        name="rnn_layer",
    )(x, vec(norm_g), w_in, conv0, h0.reshape(batch, 1, D_RNN), conv_w, vec(conv_b), ga_w, vec(ga_b),
      gx_w, vec(gx_b), vec(lam), w_out)
    return y, h_last.reshape(batch, D_RNN), c_last


def _t5_bucket(dist):
    n = jnp.maximum(dist, 0)
    max_exact = N_BUCKETS // 2
    nf = jnp.maximum(n, 1).astype(F32)
    large = max_exact + (jnp.log(nf / max_exact) / math.log(MAX_DISTANCE / max_exact)
                         * (N_BUCKETS - max_exact)).astype(jnp.int32)
    large = jnp.minimum(large, N_BUCKETS - 1)
    return jnp.where(n < max_exact, n, large)


def _bias_tables(rel_bias):
    steps = jnp.arange(N_KEYS + 1)
    period = 3 * Q_BLK
    bands, decs = [], []
    for g in range(N_GROUPS):
        heads = slice(g * HEADS_PER_GROUP, (g + 1) * HEADS_PER_GROUP)
        per_step = rel_bias[_t5_bucket(steps * DILATIONS[g])][:, heads]
        row = jnp.concatenate([per_step[::-1].T,
                               jnp.full((HEADS_PER_GROUP, period - (N_KEYS + 1)), NEG, F32)], axis=1)
        band = jnp.tile(row, (1, Q_BLK))[:, :Q_BLK * (period - 1)]
        bands.append(band.reshape(HEADS_PER_GROUP, Q_BLK, period - 1)[:, :, :2 * Q_BLK])
        decs.append(jnp.broadcast_to(per_step[::-1][:, :, None],
                                     (N_KEYS + 1, HEADS_PER_GROUP, HEAD_DIM)))
    return jnp.stack(bands, axis=1).astype(F32), jnp.stack(decs).astype(F32)


def kernel(x_prompt, x_sample, cache_kv_w128, cache_kv_w512, cache_kv_w2048, state_rglru_h,
           state_rglru_conv, attn_norm, attn_w_in, attn_q_norm, attn_k_norm, attn_w_out, rel_bias,
           rnn_norm, rnn_w_in, rnn_conv_w, rnn_conv_b, rnn_gate_a_w, rnn_gate_a_b, rnn_gate_x_w,
           rnn_gate_x_b, rnn_lambda, rnn_w_out):
    batch, seq, _ = x_prompt.shape
    bd, n_t, _ = x_sample.shape
    depth = attn_norm.shape[0] + rnn_norm.shape[0]
    caches = (cache_kv_w128, cache_kv_w512, cache_kv_w2048)

    yp = x_prompt.reshape(batch * seq, D_MODEL)
    ys = x_sample.reshape(bd * n_t, D_MODEL)
    tm_p, tm_s = 1024, bd * n_t

    bias_band, bias_dec = _bias_tables(rel_bias)
    projs_p = []
    assert depth == 4
    h_p, h_s, c_p, c_s = [], [], [], []
    for i in range(depth):
        li = i // 2
        if i % 2 == 0:
            w_in = attn_w_in[li].astype(BF16)
            w_out = attn_w_out[li].astype(BF16)
            head_scale = jnp.concatenate([
                jnp.tile(attn_q_norm[li] * (HEAD_DIM ** -0.5), N_HEADS),
                jnp.tile(attn_k_norm[li], N_HEADS)]).reshape(1, 2 * QKV_WIDTH)
            proj_p = _norm_proj(yp, attn_norm[li], w_in, head_scale, SPAN, 1024, row_orders=DILATIONS)
            proj_s = _norm_proj(ys, attn_norm[li], w_in, head_scale, tm_s, 1024, split_heads=True)
            projs_p.append(proj_p)

            u_p = _attn_prompt(proj_p, bias_band, batch, seq)
            yp = _out_proj(u_p, w_out, yp, tm_p)

            proj_s = proj_s.reshape(bd, n_t, -1, HEAD_DIM)
            if li == 0:
                u_s, new_rows0, (cache_w128, cache_w512) = _decode_attn(li, proj_s, bias_dec, caches, (0, 1))
            else:
                u_s, new_rows1, (cache_w2048,) = _decode_attn(li, proj_s, bias_dec, caches, (2,), new_rows0)
                cache_w128 = _cache_set_rows(cache_w128, new_rows1[0], li)
                cache_w512 = _cache_set_rows(cache_w512, new_rows1[1], li)
            ys = _out_proj(u_s.reshape(bd * n_t, ATTN_WIDTH), w_out, ys, tm_s)
        else:
            w_in = rnn_w_in[li].astype(BF16)
            rnn_args = (rnn_conv_w[li], rnn_conv_b[li], rnn_gate_a_w[li].astype(BF16), rnn_gate_a_b[li].reshape(-1),
                        rnn_gate_x_w[li].astype(BF16), rnn_gate_x_b[li].reshape(-1), rnn_lambda[li],
                        rnn_w_out[li].astype(BF16))
            yp, hp, cp = _rnn_layer(yp, rnn_norm[li], w_in, jnp.zeros((batch, CONV_W - 1, D_RNN), F32),
                                    jnp.zeros((batch, D_RNN), F32), *rnn_args, batch, seq, RNN_CHUNK)
            ys, hs, cs = _rnn_layer(ys, rnn_norm[li], w_in, state_rglru_conv[li], state_rglru_h[li], *rnn_args,
                                    bd, n_t, n_t)
            h_p.append(hp)
            h_s.append(hs)
            c_p.append(cp)
            c_s.append(cs)

    kv_p = [_kv_tails(projs_p, g, batch, seq) for g in range(N_GROUPS)]
    return (yp.reshape(batch, seq, D_MODEL), ys.reshape(bd, n_t, D_MODEL),
            kv_p[0], cache_w128,
            kv_p[1], cache_w512,
            kv_p[2], cache_w2048,
            jnp.stack(h_p), jnp.stack(h_s),
            jnp.stack(c_p), jnp.stack(c_s))
```

```python
import functools
import math

import jax
import jax.numpy as jnp
from jax import lax
from jax.experimental import pallas as pl
from jax.experimental.pallas import tpu as pltpu

F32 = jnp.float32
BF16 = jnp.bfloat16

D_MODEL = 1024
N_GROUPS = 3
WINDOWS = (128, 512, 2048)
DILATIONS = (1, 4, 16)
HEADS_PER_GROUP = 8
HEAD_DIM = 128
N_HEADS = N_GROUPS * HEADS_PER_GROUP
QKV_WIDTH = N_HEADS * HEAD_DIM
ATTN_WIDTH = HEADS_PER_GROUP * HEAD_DIM
Q_BLK = 128
N_KEYS = Q_BLK
SPAN = Q_BLK * DILATIONS[-1]
RNN_CHUNK = 256
COPY_CHUNK_ROWS = 256
COPY_SLOTS = 8
SCORE_KEYS = 8
SCORE_UNROLL = 4
N_BUCKETS = 32
MAX_DISTANCE = 2048
D_RNN = 1280
RNN_BLOCKS = 10
RNN_BLOCK_W = D_RNN // RNN_BLOCKS
CONV_W = 4
LRU_C = 8.0
EPS = 1e-6
NEG = -1e30

SUBLANES = 8
MXU_COLS = 256
VMEM_LIMIT = 52 * 1024 * 1024


def _params(semantics):
    return pltpu.CompilerParams(dimension_semantics=semantics, vmem_limit_bytes=VMEM_LIMIT)


def _residue_rows(start, size, stride):
    return pl.ds(start, size) if stride == 1 else pl.ds(start, size, stride=stride)


def _proj_kernel(*refs, n_norm_tiles, row_orders, tiles_per_order, n_slabs):
    x_refs, (g_ref, w_ref), rest = refs[:n_slabs], refs[n_slabs:n_slabs + 2], refs[n_slabs + 2:]
    hs_ref, o_ref, h_scr = rest
    n = pl.program_id(1)
    tm = x_refs[0].shape[0]
    k_dim = n_slabs * HEAD_DIM

    @pl.when(n == 0)
    def _():
        for i, d in enumerate(row_orders):
            for r in range(d):
                xs = [x_ref[_residue_rows(r, tm // d, d), :] for x_ref in x_refs]
                ms = jnp.sum(sum(x * x for x in xs), axis=-1, keepdims=True) * (1.0 / k_dim)
                inv = lax.rsqrt(ms + EPS)
                for c, x in enumerate(xs):
                    cols = slice(c * HEAD_DIM, (c + 1) * HEAD_DIM)
                    h_scr[i, r * (tm // d):(r + 1) * (tm // d), cols] = (x * inv * g_ref[:, cols]).astype(BF16)

    if len(row_orders) == 1:
        h = h_scr[0]
    else:
        order = jnp.where(n < tiles_per_order * len(row_orders) * 3, (n // tiles_per_order) % len(row_orders), 0)
        h = h_scr[order]
    tn = w_ref.shape[1]

    def store(head, value):
        if len(o_ref.shape) == 3:
            o_ref[:, head, :] = value
        else:
            o_ref[:, head * HEAD_DIM:(head + 1) * HEAD_DIM] = value

    @pl.when(n < n_norm_tiles)
    def _():
        for c in range(tn // MXU_COLS):
            acc = jnp.dot(h, w_ref[:, c * MXU_COLS:(c + 1) * MXU_COLS], preferred_element_type=F32)
            for hh in range(MXU_COLS // HEAD_DIM):
                head = c * (MXU_COLS // HEAD_DIM) + hh
                a = acc[:, hh * HEAD_DIM:(hh + 1) * HEAD_DIM]
                ms = jnp.mean(a * a, axis=-1, keepdims=True)
                store(head, a * lax.rsqrt(ms + EPS) * hs_ref[:, head * HEAD_DIM:(head + 1) * HEAD_DIM])

    @pl.when(n >= n_norm_tiles)
    def _():
        acc = jnp.dot(h, w_ref[...], preferred_element_type=F32)
        if len(o_ref.shape) == 3:
            for head in range(tn // HEAD_DIM):
                store(head, acc[:, head * HEAD_DIM:(head + 1) * HEAD_DIM])
        else:
            o_ref[...] = acc


def _norm_proj(x, norm_g, w, head_scale, tm, tn, row_orders=(1,), split_heads=False):
    m_rows, k_dim = x.shape
    n_cols = w.shape[1]
    n_norm_cols = head_scale.shape[1]
    assert m_rows % tm == 0 and n_cols % tn == 0 and n_norm_cols % tn == 0 and tn % MXU_COLS == 0
    assert len(row_orders) == 1 or ATTN_WIDTH % tn == 0
    n_slabs = k_dim // HEAD_DIM
    last_norm_tile = n_norm_cols // tn - 1
    if split_heads:
        out_shape = (m_rows, n_cols // HEAD_DIM, HEAD_DIM)
        out_spec = pl.BlockSpec((tm, tn // HEAD_DIM, HEAD_DIM), lambda m, n: (m, n, 0))
    else:
        out_shape = (m_rows, n_cols)
        out_spec = pl.BlockSpec((tm, tn), lambda m, n: (m, n))
    in_specs = [pl.BlockSpec((tm, HEAD_DIM), lambda m, n, c=c: (m, c), pipeline_mode=pl.Buffered(1))
                for c in range(n_slabs)] + [
        pl.BlockSpec((1, k_dim), lambda m, n: (0, 0)),
        pl.BlockSpec((k_dim, tn), lambda m, n: (0, n)),
        pl.BlockSpec((1, tn), lambda m, n: (0, jnp.minimum(n, last_norm_tile))),
    ]
    args = [x] * n_slabs + [norm_g.reshape(1, k_dim), w, head_scale]
    return pl.pallas_call(
        functools.partial(_proj_kernel, n_norm_tiles=n_norm_cols // tn, row_orders=row_orders,
                          tiles_per_order=ATTN_WIDTH // tn, n_slabs=n_slabs),
        out_shape=jax.ShapeDtypeStruct(out_shape, F32),
        grid=(m_rows // tm, n_cols // tn),
        in_specs=in_specs,
        out_specs=out_spec,
        scratch_shapes=[pltpu.VMEM((len(row_orders), tm, k_dim), BF16)],
        compiler_params=_params(("parallel", "arbitrary")),
        name="norm_proj",
    )(*args)


def _out_kernel(u_ref, w_ref, x_ref, y_ref):
    y_ref[...] = x_ref[...] + jnp.dot(u_ref[...].astype(BF16), w_ref[...], preferred_element_type=F32)


def _out_proj(u, w, x, tm):
    m_rows, k_dim = u.shape
    n_cols = w.shape[1]
    assert m_rows % tm == 0
    return pl.pallas_call(
        _out_kernel,
        out_shape=jax.ShapeDtypeStruct((m_rows, n_cols), F32),
        grid=(m_rows // tm,),
        in_specs=[
            pl.BlockSpec((tm, k_dim), lambda m: (m, 0)),
            pl.BlockSpec((k_dim, n_cols), lambda m: (0, 0)),
            pl.BlockSpec((tm, n_cols), lambda m: (m, 0)),
        ],
        out_specs=pl.BlockSpec((tm, n_cols), lambda m: (m, 0)),
        compiler_params=_params(("parallel",)),
        name="out_proj",
    )(u, w, x)


def _attn_prompt_kernel(q0, q1, q2, k0, k1, k2, v0, v1, v2, gate_ref, bias_ref, u_ref,
                        kx, vx, o_scr, l_scr):
    span = pl.program_id(2)
    q_refs = (q0, q1, q2)
    n_blk = SPAN // Q_BLK
    cur, prev = span % 2, 1 - span % 2

    @pl.when(span == 0)
    def _():
        kx[:, 1] = jnp.zeros((N_GROUPS, SPAN, HEAD_DIM), BF16)
        vx[:, 1] = jnp.zeros((N_GROUPS, SPAN, HEAD_DIM), BF16)

    for g, (k_ref, v_ref) in enumerate(((k0, v0), (k1, v1), (k2, v2))):
        kx[g, cur] = k_ref[...].astype(BF16)
        vx[g, cur] = v_ref[...].astype(BF16)

    first_span = span == 0

    def blk_rows(bi):
        return slice(bi * Q_BLK, (bi + 1) * Q_BLK)

    def prev_block(g, bi):
        per_residue = n_blk // DILATIONS[g]
        if bi % per_residue:
            return cur, bi - 1, None
        return prev, bi + per_residue - 1, first_span

    def token_rows(g, bi):
        d = DILATIONS[g]
        per_residue = n_blk // d
        return _residue_rows((bi % per_residue) * Q_BLK * d + bi // per_residue, Q_BLK, d)

    for g in range(N_GROUPS):
        each = range(n_blk)
        prevs = [prev_block(g, bi) for bi in each]
        q = [q_refs[g][blk_rows(bi), :].astype(BF16) for bi in each]
        k = [jnp.concatenate([kx[g, prevs[bi][0], blk_rows(prevs[bi][1]), :], kx[g, cur, blk_rows(bi), :]], axis=0)
             for bi in each]
        v = [jnp.concatenate([vx[g, prevs[bi][0], blk_rows(prevs[bi][1]), :], vx[g, cur, blk_rows(bi), :]], axis=0)
             for bi in each]
        s = [lax.dot_general(q[bi], k[bi], (((1,), (1,)), ((), ())), preferred_element_type=F32) + bias_ref[g]
             for bi in each]
        in_prev = lax.broadcasted_iota(jnp.int32, s[0].shape, 1) < Q_BLK
        s = [s[bi] if prevs[bi][2] is None else jnp.where(jnp.logical_and(prevs[bi][2], in_prev), NEG, s[bi])
             for bi in each]
        m = [jnp.max(s[bi], axis=-1, keepdims=True) for bi in each]
        p = [jnp.exp(s[bi] - m[bi]) for bi in each]
        l = [jnp.sum(p[bi], axis=-1, keepdims=True) for bi in each]
        o = [jnp.dot(p[bi].astype(BF16), v[bi], preferred_element_type=F32) / l[bi] for bi in each]
        for bi in each:
            o_scr[g, token_rows(g, bi), :] = o[bi]
            l_scr[g, token_rows(g, bi), :] = jnp.broadcast_to(m[bi] + jnp.log(l[bi]), (Q_BLK, HEAD_DIM))

    for c in range(n_blk):
        rows_ = blk_rows(c)
        l0, l1, l2 = l_scr[0, rows_, :], l_scr[1, rows_, :], l_scr[2, rows_, :]
        mx = jnp.maximum(jnp.maximum(l0, l1), l2)
        e0, e1, e2 = jnp.exp(l0 - mx), jnp.exp(l1 - mx), jnp.exp(l2 - mx)
        merged = (e0 * o_scr[0, rows_, :] + e1 * o_scr[1, rows_, :] + e2 * o_scr[2, rows_, :]) / (e0 + e1 + e2)
        gate = gate_ref[rows_, :]
        u_ref[rows_, :] = gate * jax.nn.sigmoid(gate) * merged


def _attn_prompt(proj, bias_band, batch, seq):
    n_span = seq // SPAN
    assert seq % SPAN == 0

    def col_spec(first_col_block):
        return pl.BlockSpec((SPAN, HEAD_DIM),
                            lambda b, h, s, c=first_col_block: (b * n_span + s, c + h))

    head_blocks = QKV_WIDTH // HEAD_DIM
    in_specs = (
        [col_spec(g * HEADS_PER_GROUP) for g in range(N_GROUPS)]
        + [col_spec(head_blocks + g * HEADS_PER_GROUP) for g in range(N_GROUPS)]
        + [col_spec(2 * head_blocks + g * HEADS_PER_GROUP) for g in range(N_GROUPS)]
        + [col_spec(3 * head_blocks)]
        + [pl.BlockSpec((None, N_GROUPS, Q_BLK, 2 * Q_BLK), lambda b, h, s: (h, 0, 0, 0))]
    )
    return pl.pallas_call(
        _attn_prompt_kernel,
        out_shape=jax.ShapeDtypeStruct((batch * seq, ATTN_WIDTH), F32),
        grid=(batch, HEADS_PER_GROUP, n_span),
        in_specs=in_specs,
        out_specs=pl.BlockSpec((SPAN, HEAD_DIM), lambda b, h, s: (b * n_span + s, h)),
        scratch_shapes=[
            pltpu.VMEM((N_GROUPS, 2, SPAN, HEAD_DIM), BF16),
            pltpu.VMEM((N_GROUPS, 2, SPAN, HEAD_DIM), BF16),
            pltpu.VMEM((N_GROUPS, SPAN, HEAD_DIM), F32),
            pltpu.VMEM((N_GROUPS, SPAN, HEAD_DIM), F32),
        ],
        compiler_params=_params(("parallel", "parallel", "arbitrary")),
        name="attn_prompt",
    )(*([proj] * 10), bias_band)


def _decode_kernel(*refs, layer, hosted, stale_other):
    n_h = len(hosted)
    (q_ref, kn_ref, vn_ref, gate_ref, bias_ref, c0_ref, c1_ref, c2_ref), refs = refs[:8], refs[8:]
    cache_hbm, refs = refs[:n_h], refs[n_h:]
    other_new, refs = (refs[:0], refs) if stale_other else (refs[:n_h], refs[n_h:])
    (u_ref, o0_ref, o1_ref, o2_ref), refs = refs[:4], refs[4:]
    out_hbm, (s_scr, buf, rows_scr, in_sems, out_sems, row_sems) = refs[:n_h], refs[n_h:]
    cache_refs = (c0_ref, c1_ref, c2_ref)
    new_refs = (o0_ref, o1_ref, o2_ref)
    n_t = q_ref.shape[0]
    b = pl.program_id(0)

    for g, o_ref in enumerate(new_refs):
        heads = slice(g * HEADS_PER_GROUP, (g + 1) * HEADS_PER_GROUP)
        o_ref[:, 0] = kn_ref[:, heads, :]
        o_ref[:, 1] = vn_ref[:, heads, :]

    n_slots, chunk_rows = buf.shape[:2]
    chunks = []
    for l in range(2):
        for hi, g in enumerate(hosted):
            kept = WINDOWS[g] - n_t
            spans = [(n_t + r0, r0, min(chunk_rows, kept - r0)) for r0 in range(0, kept, chunk_rows)]
            if stale_other and l != layer:
                spans.append((kept, kept, n_t))
            for src0, dst0, n in spans:
                slot = len(chunks) % n_slots
                chunks.append((
                    pltpu.make_async_copy(cache_hbm[hi].at[l, b, pl.ds(src0, n)], buf.at[slot, pl.ds(0, n)],
                                          in_sems.at[slot]),
                    pltpu.make_async_copy(buf.at[slot, pl.ds(0, n)], out_hbm[hi].at[l, b, pl.ds(dst0, n)],
                                          out_sems.at[slot])))
    row_copies = []
    for hi, g in enumerate(hosted):
        newest = pl.ds(WINDOWS[g] - n_t, n_t)
        rows_scr[2 * hi] = new_refs[g][...]
        row_copies.append(pltpu.make_async_copy(rows_scr.at[2 * hi], out_hbm[hi].at[layer, b, newest],
                                                row_sems.at[2 * hi]))
        if not stale_other:
            rows_scr[2 * hi + 1] = other_new[hi][...]
            row_copies.append(pltpu.make_async_copy(rows_scr.at[2 * hi + 1], out_hbm[hi].at[1 - layer, b, newest],
                                                    row_sems.at[2 * hi + 1]))
    for cp_in, _ in chunks[:n_slots]:
        cp_in.start()
    for cp in row_copies:
        cp.start()
    pumped = [0]

    all_resident = len(chunks) <= n_slots

    def pump():
        i = pumped[0]
        if all_resident or i == len(chunks):
            return
        chunks[i][0].wait()
        chunks[i][1].start()
        if i:
            chunks[i - 1][1].wait()
            if i - 1 + n_slots < len(chunks):
                chunks[i - 1 + n_slots][0].start()
        pumped[0] = i + 1

    outs, lses = [], []
    for g in range(N_GROUPS):
        d = DILATIONS[g]
        heads = slice(g * HEADS_PER_GROUP, (g + 1) * HEADS_PER_GROUP)
        c_ref = cache_refs[g]
        n_cache = N_KEYS - (n_t - 1) // d
        q_t = [q_ref[t, heads, :] for t in range(n_t)]

        def key_tile(kv, t, i, g=g, d=d, c_ref=c_ref, heads=heads):
            row = t + i * d
            if row < WINDOWS[g]:
                return c_ref[row // d, row % d, kv]
            return (kn_ref, vn_ref)[kv][row - WINDOWS[g], heads, :]

        def store_scores(keys, tiles, g=g):
            prod = jnp.concatenate([q_t[t] * tiles[ii][t] for ii in range(len(keys)) for t in range(n_t)], axis=0)
            sums = jnp.dot(prod.astype(BF16), jnp.ones((HEAD_DIM, HEAD_DIM), BF16), preferred_element_type=F32)
            for ii, i in enumerate(keys):
                for t in range(n_t):
                    row0 = (ii * n_t + t) * HEADS_PER_GROUP
                    s_scr[t, i] = sums[row0:row0 + HEADS_PER_GROUP] + bias_ref[g, i]

        def scores_body(it, carry, d=d, c_ref=c_ref, store_scores=store_scores):
            keys = [it * SCORE_KEYS + ii for ii in range(SCORE_KEYS)]
            store_scores(keys, [[c_ref[i + t // d, t % d, 0] for t in range(n_t)] for i in keys])
            return carry
        lax.fori_loop(0, n_cache // SCORE_KEYS, scores_body, 0, unroll=SCORE_UNROLL)
        pump()
        rest = list(range(n_cache // SCORE_KEYS * SCORE_KEYS, N_KEYS + 1))
        store_scores(rest, [[key_tile(0, t, i) for t in range(n_t)] for i in rest])
        pump()

        m_t, l_t = [], []
        for t in range(n_t):
            sv = s_scr[t, 0:N_KEYS + 1]
            m = jnp.max(sv, axis=0)
            p = jnp.exp(sv - m)
            s_scr[t, 0:N_KEYS + 1] = p
            m_t.append(m)
            l_t.append(jnp.sum(p, axis=0))
            if t % 2:
                pump()

        def pv_body(i, accs, d=d, c_ref=c_ref):
            return tuple(accs[t] + s_scr[t, i] * c_ref[i + t // d, t % d, 1] for t in range(n_t))
        accs = lax.fori_loop(0, n_cache, pv_body,
                             tuple(jnp.zeros((HEADS_PER_GROUP, HEAD_DIM), F32) for _ in range(n_t)))
        pump()
        accs = list(accs)
        for i in range(n_cache, N_KEYS + 1):
            for t in range(n_t):
                accs[t] = accs[t] + s_scr[t, i] * key_tile(1, t, i)
        outs.append([accs[t] / l_t[t] for t in range(n_t)])
        lses.append([m_t[t] + jnp.log(l_t[t]) for t in range(n_t)])
        if all_resident and g == N_GROUPS - 2:
            for cp_in, cp_out in chunks:
                cp_in.wait()
                cp_out.start()

    for t in range(n_t):
        l0, l1, l2 = lses[0][t], lses[1][t], lses[2][t]
        mx = jnp.maximum(jnp.maximum(l0, l1), l2)
        e0, e1, e2 = jnp.exp(l0 - mx), jnp.exp(l1 - mx), jnp.exp(l2 - mx)
        merged = (e0 * outs[0][t] + e1 * outs[1][t] + e2 * outs[2][t]) / (e0 + e1 + e2)
        gate = gate_ref[t]
        u_ref[t] = gate * jax.nn.sigmoid(gate) * merged

    if all_resident:
        for _, cp_out in chunks:
            cp_out.wait()
    else:
        while pumped[0] < len(chunks):
            pump()
        chunks[-1][1].wait()
    for cp in row_copies:
        cp.wait()


def _decode_attn(layer, proj, bias_dec, caches, hosted, other_new=None):
    bd, n_t = proj.shape[:2]
    assert n_t == SUBLANES and all(c.shape[0] == 2 for c in caches)
    stale_other = other_new is None
    views, view_specs = [], []
    for g, cache in enumerate(caches):
        d = DILATIONS[g]
        n_res = min(d, n_t)
        views.append(cache.reshape(cache.shape[0], bd, N_KEYS, d, 2, HEADS_PER_GROUP, HEAD_DIM))
        view_specs.append(pl.BlockSpec((None, None, N_KEYS, n_res, 2, HEADS_PER_GROUP, HEAD_DIM),
                                       lambda b: (layer, b, 0, 0, 0, 0, 0)))
    qkv_specs = [pl.BlockSpec((None, n_t, N_HEADS, HEAD_DIM), lambda b, i=i: (b, 0, i, 0)) for i in range(3)]
    gate_spec = pl.BlockSpec((None, n_t, HEADS_PER_GROUP, HEAD_DIM),
                             lambda b: (b, 0, 3 * N_HEADS // HEADS_PER_GROUP, 0))
    u_spec = pl.BlockSpec((None, n_t, HEADS_PER_GROUP, HEAD_DIM), lambda b: (b, 0, 0, 0))
    new_rows_shape = (bd, n_t, 2, HEADS_PER_GROUP, HEAD_DIM)
    new_rows_spec = pl.BlockSpec((None,) + new_rows_shape[1:], lambda b: (b, 0, 0, 0, 0))
    any_spec = pl.BlockSpec(memory_space=pl.ANY)
    hosted_caches = [caches[g] for g in hosted]
    others = [] if stale_other else [other_new[g] for g in hosted]
    res = pl.pallas_call(
        functools.partial(_decode_kernel, layer=layer, hosted=tuple(hosted), stale_other=stale_other),
        out_shape=[jax.ShapeDtypeStruct((bd, n_t, HEADS_PER_GROUP, HEAD_DIM), F32)]
        + [jax.ShapeDtypeStruct(new_rows_shape, c.dtype) for c in caches]
        + [jax.ShapeDtypeStruct(c.shape, c.dtype) for c in hosted_caches],
        grid=(bd,),
        in_specs=qkv_specs + [gate_spec, pl.BlockSpec(bias_dec.shape, lambda b: (0, 0, 0, 0))] + view_specs
        + [any_spec] * len(hosted) + [new_rows_spec] * len(others),
        out_specs=[u_spec] + [new_rows_spec] * N_GROUPS + [any_spec] * len(hosted),
        scratch_shapes=[
            pltpu.VMEM((n_t, N_KEYS + SUBLANES, HEADS_PER_GROUP, HEAD_DIM), F32),
            pltpu.VMEM((COPY_SLOTS, COPY_CHUNK_ROWS, 2, HEADS_PER_GROUP, HEAD_DIM), F32),
            pltpu.VMEM((2 * len(hosted), n_t, 2, HEADS_PER_GROUP, HEAD_DIM), F32),
            pltpu.SemaphoreType.DMA((COPY_SLOTS,)),
            pltpu.SemaphoreType.DMA((COPY_SLOTS,)),
            pltpu.SemaphoreType.DMA((2 * len(hosted),)),
        ],
        compiler_params=_params(("parallel",)),
        name="decode_attn",
    )(proj, proj, proj, proj, bias_dec, *views, *hosted_caches, *others)
    return res[0], tuple(res[1:1 + N_GROUPS]), tuple(res[1 + N_GROUPS:])


def _kv_tail_kernel(p0_ref, p1_ref, o_ref):
    layer = pl.program_id(0)
    for p_ref, which in ((p0_ref, 0), (p1_ref, 1)):
        @pl.when(layer == which)
        def _(p_ref=p_ref):
            for h in range(HEADS_PER_GROUP):
                o_ref[:, h, :] = p_ref[:, h * HEAD_DIM:(h + 1) * HEAD_DIM]


def _kv_tails(projs, g, batch, seq):
    d = DILATIONS[g]
    keep = min(WINDOWS[g], seq)
    per_res = keep // d
    assert len(projs) == 2 and keep <= SPAN and keep % d == 0 and (SPAN // d) % per_res == 0
    assert seq % per_res == 0 and per_res % SUBLANES == 0

    def in_rows(layer, b, kv, r):
        first = (b + 1) * seq - SPAN + r * (SPAN // d) + (SPAN - keep) // d
        return first // per_res, (1 + kv) * (QKV_WIDTH // ATTN_WIDTH) + g

    out = pl.pallas_call(
        _kv_tail_kernel,
        out_shape=jax.ShapeDtypeStruct((len(projs), batch, per_res, d, 2, HEADS_PER_GROUP, HEAD_DIM), F32),
        grid=(len(projs), batch, 2, d),
        in_specs=[pl.BlockSpec((per_res, ATTN_WIDTH), in_rows)] * 2,
        out_specs=pl.BlockSpec((None, None, per_res, None, None, HEADS_PER_GROUP, HEAD_DIM),
                               lambda layer, b, kv, r: (layer, b, 0, r, kv, 0, 0)),
        compiler_params=_params(("parallel",) * 4),
        name="kv_tails",
    )(*projs)
    return out.reshape(len(projs), batch, keep, 2, HEADS_PER_GROUP, HEAD_DIM)


def _set_rows_kernel(new_ref, cache_ref, o_ref):
    del cache_ref
    o_ref[...] = new_ref[...]


def _cache_set_rows(cache, new_rows, layer):
    batch, n_new = new_rows.shape[:2]
    tile = new_rows.shape[2:]
    zeros = (0,) * len(tile)
    last_block = cache.shape[2] // n_new - 1
    assert cache.shape[2] % n_new == 0
    return pl.pallas_call(
        _set_rows_kernel,
        out_shape=jax.ShapeDtypeStruct(cache.shape, cache.dtype),
        grid=(batch,),
        in_specs=[pl.BlockSpec((None, n_new) + tile, lambda b: (b, 0) + zeros),
                  pl.BlockSpec(memory_space=pl.ANY)],
        out_specs=pl.BlockSpec((None, None, n_new) + tile, lambda b: (layer, b, last_block) + zeros),
        input_output_aliases={1: 0},
        compiler_params=_params(("parallel",)),
        name="cache_set_rows",
    )(new_rows, cache)


def _scan_rows(a, b, h_prev):
    n_tiles = a.shape[0] // SUBLANES
    a3 = a.reshape(n_tiles, SUBLANES, a.shape[1])
    b3 = b.reshape(n_tiles, SUBLANES, b.shape[1])
    row = lax.broadcasted_iota(jnp.int32, a3.shape, 1)
    shift = 1
    while shift < SUBLANES:
        a_sh = pltpu.roll(a3, shift, axis=1)
        b_sh = pltpu.roll(b3, shift, axis=1)
        keep = row >= shift
        b3 = jnp.where(keep, a3 * b_sh + b3, b3)
        a3 = jnp.where(keep, a3 * a_sh, a3)
        shift *= 2
    tiles = []
    carry = h_prev
    for k in range(n_tiles):
        h_k = b3[k] + a3[k] * carry
        carry = h_k[SUBLANES - 1:SUBLANES]
        tiles.append(h_k)
    return jnp.concatenate(tiles, axis=0), carry


def _rnn_kernel(x_ref, ng_ref, win_ref, conv0_ref, h0_ref, cw_ref, cb_ref, gaw_ref, gab_ref,
                gxw_ref, gxb_ref, lam_ref, wout_ref, y_ref, hlast_ref, clast_ref,
                xe_scr, gate_scr, h_scr, u_scr):
    c = pl.program_id(1)
    tl = x_ref.shape[0]
    pad = SUBLANES

    @pl.when(c == 0)
    def _():
        xe_scr[pad - (CONV_W - 1):pad] = conv0_ref[...]
        h_scr[...] = h0_ref[...]

    @pl.when(c > 0)
    def _():
        xe_scr[0:pad] = xe_scr[tl:tl + pad]

    x = x_ref[...]
    ms = jnp.mean(x * x, axis=-1, keepdims=True)
    hn = (x * lax.rsqrt(ms + EPS) * ng_ref[...]).astype(BF16)
    xe_scr[pad:pad + tl] = jnp.dot(hn, win_ref[:, :D_RNN], preferred_element_type=F32)
    gate_scr[...] = jnp.dot(hn, win_ref[:, D_RNN:], preferred_element_type=F32)
    clast_ref[...] = xe_scr[pad + tl - (CONV_W - 1):pad + tl]

    for j in range(RNN_BLOCKS):
        cols = slice(j * RNN_BLOCK_W, (j + 1) * RNN_BLOCK_W)
        xc = cb_ref[:, cols]
        for k in range(CONV_W):
            start = pad - (CONV_W - 1) + k
            xc = xc + cw_ref[k:k + 1, cols] * xe_scr[start:start + tl, cols]
        xcb = xc.astype(BF16)
        r = jax.nn.sigmoid(jnp.dot(xcb, gaw_ref[j], preferred_element_type=F32) + gab_ref[:, cols])
        ig = jax.nn.sigmoid(jnp.dot(xcb, gxw_ref[j], preferred_element_type=F32) + gxb_ref[:, cols])
        neg_lam = -lam_ref[:, cols]
        softplus = jnp.maximum(neg_lam, 0.0) + jnp.log1p(jnp.exp(-jnp.abs(neg_lam)))
        log_a = -LRU_C * r * softplus
        a = jnp.exp(log_a)
        bterm = jnp.sqrt(-jnp.tanh(log_a) * (a * a + 1.0)) * (ig * xc)
        h, h_end = _scan_rows(a, bterm, h_scr[:, cols])
        h_scr[:, cols] = h_end
        gate = gate_scr[:, cols]
        u_scr[:, cols] = gate * jax.nn.sigmoid(gate) * h

    hlast_ref[...] = h_scr[...]
    y_ref[...] = x_ref[...] + jnp.dot(u_scr[...].astype(BF16), wout_ref[...], preferred_element_type=F32)


def _rnn_layer(x, norm_g, w_in, conv0, h0, conv_w, conv_b, ga_w, ga_b, gx_w, gx_b, lam, w_out, batch, seq, tl):
    n_chunks = seq // tl
    assert seq % tl == 0 and tl % SUBLANES == 0
    row = lambda b, c: (b * n_chunks + c, 0)
    vec = lambda a: a.reshape(1, -1)
    full = lambda shape: pl.BlockSpec(shape, lambda b, c: (0,) * len(shape))
    per_batch = lambda rows_: pl.BlockSpec((None, rows_, D_RNN), lambda b, c: (b, 0, 0))
    y, h_last, c_last = pl.pallas_call(
        _rnn_kernel,
        out_shape=[jax.ShapeDtypeStruct((batch * seq, D_MODEL), F32),
                   jax.ShapeDtypeStruct((batch, 1, D_RNN), F32),
                   jax.ShapeDtypeStruct((batch, CONV_W - 1, D_RNN), F32)],
        grid=(batch, n_chunks),
        in_specs=[
            pl.BlockSpec((tl, D_MODEL), row),
            full((1, D_MODEL)),
            full((D_MODEL, 2 * D_RNN)),
            per_batch(CONV_W - 1),
            per_batch(1),
            full((CONV_W, D_RNN)),
            full((1, D_RNN)),
            full((RNN_BLOCKS, RNN_BLOCK_W, RNN_BLOCK_W)),
            full((1, D_RNN)),
            full((RNN_BLOCKS, RNN_BLOCK_W, RNN_BLOCK_W)),
            full((1, D_RNN)),
            full((1, D_RNN)),
            full((D_RNN, D_MODEL)),
        ],
        out_specs=[pl.BlockSpec((tl, D_MODEL), row), per_batch(1), per_batch(CONV_W - 1)],
        scratch_shapes=[
            pltpu.VMEM((tl + SUBLANES, D_RNN), F32),
            pltpu.VMEM((tl, D_RNN), F32),
            pltpu.VMEM((1, D_RNN), F32),
            pltpu.VMEM((tl, D_RNN), F32),
        ],
        compiler_params=_params(("parallel", "arbitrary")),
        name="rnn_layer",
    )(x, vec(norm_g), w_in, conv0, h0.reshape(batch, 1, D_RNN), conv_w, vec(conv_b), ga_w, vec(ga_b),
      gx_w, vec(gx_b), vec(lam), w_out)
    return y, h_last.reshape(batch, D_RNN), c_last


def _t5_bucket(dist):
    n = jnp.maximum(dist, 0)
    max_exact = N_BUCKETS // 2
    nf = jnp.maximum(n, 1).astype(F32)
    large = max_exact + (jnp.log(nf / max_exact) / math.log(MAX_DISTANCE / max_exact)
                         * (N_BUCKETS - max_exact)).astype(jnp.int32)
    large = jnp.minimum(large, N_BUCKETS - 1)
    return jnp.where(n < max_exact, n, large)


def _bias_tables(rel_bias):
    steps = jnp.arange(N_KEYS + 1)
    period = 3 * Q_BLK
    bands, decs = [], []
    for g in range(N_GROUPS):
        heads = slice(g * HEADS_PER_GROUP, (g + 1) * HEADS_PER_GROUP)
        per_step = rel_bias[_t5_bucket(steps * DILATIONS[g])][:, heads]
        row = jnp.concatenate([per_step[::-1].T,
                               jnp.full((HEADS_PER_GROUP, period - (N_KEYS + 1)), NEG, F32)], axis=1)
        band = jnp.tile(row, (1, Q_BLK))[:, :Q_BLK * (period - 1)]
        bands.append(band.reshape(HEADS_PER_GROUP, Q_BLK, period - 1)[:, :, :2 * Q_BLK])
        decs.append(jnp.broadcast_to(per_step[::-1][:, :, None],
                                     (N_KEYS + 1, HEADS_PER_GROUP, HEAD_DIM)))
    return jnp.stack(bands, axis=1).astype(F32), jnp.stack(decs).astype(F32)


def kernel(x_prompt, x_sample, cache_kv_w128, cache_kv_w512, cache_kv_w2048, state_rglru_h,
           state_rglru_conv, attn_norm, attn_w_in, attn_q_norm, attn_k_norm, attn_w_out, rel_bias,
           rnn_norm, rnn_w_in, rnn_conv_w, rnn_conv_b, rnn_gate_a_w, rnn_gate_a_b, rnn_gate_x_w,
           rnn_gate_x_b, rnn_lambda, rnn_w_out):
    batch, seq, _ = x_prompt.shape
    bd, n_t, _ = x_sample.shape
    depth = attn_norm.shape[0] + rnn_norm.shape[0]
    caches = (cache_kv_w128, cache_kv_w512, cache_kv_w2048)

    yp = x_prompt.reshape(batch * seq, D_MODEL)
    ys = x_sample.reshape(bd * n_t, D_MODEL)
    tm_p, tm_s = 1024, bd * n_t

    bias_band, bias_dec = _bias_tables(rel_bias)
    projs_p = []
    assert depth == 4
    h_p, h_s, c_p, c_s = [], [], [], []
    for i in range(depth):
        li = i // 2
        if i % 2 == 0:
            w_in = attn_w_in[li].astype(BF16)
            w_out = attn_w_out[li].astype(BF16)
            head_scale = jnp.concatenate([
                jnp.tile(attn_q_norm[li] * (HEAD_DIM ** -0.5), N_HEADS),
                jnp.tile(attn_k_norm[li], N_HEADS)]).reshape(1, 2 * QKV_WIDTH)
            proj_p = _norm_proj(yp, attn_norm[li], w_in, head_scale, SPAN, 1024, row_orders=DILATIONS)
            proj_s = _norm_proj(ys, attn_norm[li], w_in, head_scale, tm_s, 1024, split_heads=True)
            projs_p.append(proj_p)

            u_p = _attn_prompt(proj_p, bias_band, batch, seq)
            yp = _out_proj(u_p, w_out, yp, tm_p)

            proj_s = proj_s.reshape(bd, n_t, -1, HEAD_DIM)
            if li == 0:
                u_s, new_rows0, (cache_w128, cache_w512) = _decode_attn(li, proj_s, bias_dec, caches, (0, 1))
            else:
                u_s, new_rows1, (cache_w2048,) = _decode_attn(li, proj_s, bias_dec, caches, (2,), new_rows0)
                cache_w128 = _cache_set_rows(cache_w128, new_rows1[0], li)
                cache_w512 = _cache_set_rows(cache_w512, new_rows1[1], li)
            ys = _out_proj(u_s.reshape(bd * n_t, ATTN_WIDTH), w_out, ys, tm_s)
        else:
            w_in = rnn_w_in[li].astype(BF16)
            rnn_args = (rnn_conv_w[li], rnn_conv_b[li], rnn_gate_a_w[li].astype(BF16), rnn_gate_a_b[li].reshape(-1),
                        rnn_gate_x_w[li].astype(BF16), rnn_gate_x_b[li].reshape(-1), rnn_lambda[li],
                        rnn_w_out[li].astype(BF16))
            yp, hp, cp = _rnn_layer(yp, rnn_norm[li], w_in, jnp.zeros((batch, CONV_W - 1, D_RNN), F32),
                                    jnp.zeros((batch, D_RNN), F32), *rnn_args, batch, seq, RNN_CHUNK)
            ys, hs, cs = _rnn_layer(ys, rnn_norm[li], w_in, state_rglru_conv[li], state_rglru_h[li], *rnn_args,
                                    bd, n_t, n_t)
            h_p.append(hp)
            h_s.append(hs)
            c_p.append(cp)
            c_s.append(cs)

    kv_p = [_kv_tails(projs_p, g, batch, seq) for g in range(N_GROUPS)]
    return (yp.reshape(batch, seq, D_MODEL), ys.reshape(bd, n_t, D_MODEL),
            kv_p[0], cache_w128,
            kv_p[1], cache_w512,
            kv_p[2], cache_w2048,
            jnp.stack(h_p), jnp.stack(h_s),
            jnp.stack(c_p), jnp.stack(c_s))
```

```python
import functools
import math

import jax
import jax.numpy as jnp
from jax import lax
from jax.experimental import pallas as pl
from jax.experimental.pallas import tpu as pltpu

F32 = jnp.float32
BF16 = jnp.bfloat16

D_MODEL = 1024
N_GROUPS = 3
WINDOWS = (128, 512, 2048)
DILATIONS = (1, 4, 16)
HEADS_PER_GROUP = 8
HEAD_DIM = 128
N_HEADS = N_GROUPS * HEADS_PER_GROUP
QKV_WIDTH = N_HEADS * HEAD_DIM
ATTN_WIDTH = HEADS_PER_GROUP * HEAD_DIM
Q_BLK = 128
N_KEYS = Q_BLK
SPAN = Q_BLK * DILATIONS[-1]
BLOCKS_IN_FLIGHT = 8
PROJ_COLS = 1024
OUT_ROWS = 1024
RNN_CHUNK = 256
COPY_CHUNK_ROWS = 256
COPY_SLOTS = 8
SCORE_KEYS = 8
SCORE_UNROLL = 4
N_BUCKETS = 32
MAX_DISTANCE = 2048
D_RNN = 1280
RNN_BLOCKS = 10
RNN_BLOCK_W = D_RNN // RNN_BLOCKS
CONV_W = 4
LRU_C = 8.0
EPS = 1e-6
NEG = -1e30

SUBLANES = 8
MXU_COLS = 256
VMEM_LIMIT = 52 * 1024 * 1024


def _params(semantics):
    return pltpu.CompilerParams(dimension_semantics=semantics, vmem_limit_bytes=VMEM_LIMIT)


def _residue_rows(start, size, stride):
    return pl.ds(start, size) if stride == 1 else pl.ds(start, size, stride=stride)


def _proj_kernel(*refs, n_norm_tiles, row_orders, tiles_per_order, n_slabs):
    x_refs, (g_ref, w_ref), rest = refs[:n_slabs], refs[n_slabs:n_slabs + 2], refs[n_slabs + 2:]
    hs_ref, o_ref, h_scr = rest
    n = pl.program_id(1)
    tm = x_refs[0].shape[0]
    k_dim = n_slabs * HEAD_DIM

    @pl.when(n == 0)
    def _():
        for i, d in enumerate(row_orders):
            for r in range(d):
                xs = [x_ref[_residue_rows(r, tm // d, d), :] for x_ref in x_refs]
                ms = jnp.sum(sum(x * x for x in xs), axis=-1, keepdims=True) * (1.0 / k_dim)
                inv = lax.rsqrt(ms + EPS)
                for c, x in enumerate(xs):
                    cols = slice(c * HEAD_DIM, (c + 1) * HEAD_DIM)
                    h_scr[i, r * (tm // d):(r + 1) * (tm // d), cols] = (x * inv * g_ref[:, cols]).astype(BF16)

    if len(row_orders) == 1:
        h = h_scr[0]
    else:
        order = jnp.where(n < tiles_per_order * len(row_orders) * 3, (n // tiles_per_order) % len(row_orders), 0)
        h = h_scr[order]
    tn = w_ref.shape[1]

    def store(head, value):
        if len(o_ref.shape) == 3:
            o_ref[:, head, :] = value
        else:
            o_ref[:, head * HEAD_DIM:(head + 1) * HEAD_DIM] = value

    @pl.when(n < n_norm_tiles)
    def _():
        for c in range(tn // MXU_COLS):
            acc = jnp.dot(h, w_ref[:, c * MXU_COLS:(c + 1) * MXU_COLS], preferred_element_type=F32)
            for hh in range(MXU_COLS // HEAD_DIM):
                head = c * (MXU_COLS // HEAD_DIM) + hh
                a = acc[:, hh * HEAD_DIM:(hh + 1) * HEAD_DIM]
                ms = jnp.mean(a * a, axis=-1, keepdims=True)
                store(head, a * lax.rsqrt(ms + EPS) * hs_ref[:, head * HEAD_DIM:(head + 1) * HEAD_DIM])

    @pl.when(n >= n_norm_tiles)
    def _():
        acc = jnp.dot(h, w_ref[...], preferred_element_type=F32)
        if len(o_ref.shape) == 3:
            for head in range(tn // HEAD_DIM):
                store(head, acc[:, head * HEAD_DIM:(head + 1) * HEAD_DIM])
        else:
            o_ref[...] = acc


def _norm_proj(x, norm_g, w, head_scale, tm, tn, row_orders=(1,), split_heads=False):
    m_rows, k_dim = x.shape
    n_cols = w.shape[1]
    n_norm_cols = head_scale.shape[1]
    assert m_rows % tm == 0 and n_cols % tn == 0 and n_norm_cols % tn == 0 and tn % MXU_COLS == 0
    assert len(row_orders) == 1 or ATTN_WIDTH % tn == 0
    n_slabs = k_dim // HEAD_DIM
    last_norm_tile = n_norm_cols // tn - 1
    if split_heads:
        out_shape = (m_rows, n_cols // HEAD_DIM, HEAD_DIM)
        out_spec = pl.BlockSpec((tm, tn // HEAD_DIM, HEAD_DIM), lambda m, n: (m, n, 0))
    else:
        out_shape = (m_rows, n_cols)
        out_spec = pl.BlockSpec((tm, tn), lambda m, n: (m, n))
    in_specs = [pl.BlockSpec((tm, HEAD_DIM), lambda m, n, c=c: (m, c), pipeline_mode=pl.Buffered(1))
                for c in range(n_slabs)] + [
        pl.BlockSpec((1, k_dim), lambda m, n: (0, 0)),
        pl.BlockSpec((k_dim, tn), lambda m, n: (0, n)),
        pl.BlockSpec((1, tn), lambda m, n: (0, jnp.minimum(n, last_norm_tile))),
    ]
    args = [x] * n_slabs + [norm_g.reshape(1, k_dim), w, head_scale]
    return pl.pallas_call(
        functools.partial(_proj_kernel, n_norm_tiles=n_norm_cols // tn, row_orders=row_orders,
                          tiles_per_order=ATTN_WIDTH // tn, n_slabs=n_slabs),
        out_shape=jax.ShapeDtypeStruct(out_shape, F32),
        grid=(m_rows // tm, n_cols // tn),
        in_specs=in_specs,
        out_specs=out_spec,
        scratch_shapes=[pltpu.VMEM((len(row_orders), tm, k_dim), BF16)],
        compiler_params=_params(("parallel", "arbitrary")),
        name="norm_proj",
    )(*args)


def _out_kernel(u_ref, w_ref, x_ref, y_ref):
    y_ref[...] = x_ref[...] + jnp.dot(u_ref[...].astype(BF16), w_ref[...], preferred_element_type=F32)


def _out_proj(u, w, x, tm):
    m_rows, k_dim = u.shape
    n_cols = w.shape[1]
    assert m_rows % tm == 0
    return pl.pallas_call(
        _out_kernel,
        out_shape=jax.ShapeDtypeStruct((m_rows, n_cols), F32),
        grid=(m_rows // tm,),
        in_specs=[
            pl.BlockSpec((tm, k_dim), lambda m: (m, 0)),
            pl.BlockSpec((k_dim, n_cols), lambda m: (0, 0)),
            pl.BlockSpec((tm, n_cols), lambda m: (m, 0)),
        ],
        out_specs=pl.BlockSpec((tm, n_cols), lambda m: (m, 0)),
        compiler_params=_params(("parallel",)),
        name="out_proj",
    )(u, w, x)


def _attn_prompt_kernel(q0, q1, q2, k0, k1, k2, v0, v1, v2, gate_ref, bias_ref, u_ref,
                        kx, vx, o_scr, l_scr):
    span = pl.program_id(2)
    q_refs = (q0, q1, q2)
    n_blk = SPAN // Q_BLK
    cur, prev = span % 2, 1 - span % 2

    @pl.when(span == 0)
    def _():
        kx[:, 1] = jnp.zeros((N_GROUPS, SPAN, HEAD_DIM), BF16)
        vx[:, 1] = jnp.zeros((N_GROUPS, SPAN, HEAD_DIM), BF16)

    for g, (k_ref, v_ref) in enumerate(((k0, v0), (k1, v1), (k2, v2))):
        kx[g, cur] = k_ref[...].astype(BF16)
        vx[g, cur] = v_ref[...].astype(BF16)

    first_span = span == 0

    def blk_rows(bi):
        return slice(bi * Q_BLK, (bi + 1) * Q_BLK)

    def prev_block(g, bi):
        per_residue = n_blk // DILATIONS[g]
        if bi % per_residue:
            return cur, bi - 1, None
        return prev, bi + per_residue - 1, first_span

    def token_rows(g, bi):
        d = DILATIONS[g]
        per_residue = n_blk // d
        return _residue_rows((bi % per_residue) * Q_BLK * d + bi // per_residue, Q_BLK, d)

    def band_blocks(g, blocks):
        prevs = {bi: prev_block(g, bi) for bi in blocks}
        q = {bi: q_refs[g][blk_rows(bi), :].astype(BF16) for bi in blocks}
        k = {bi: jnp.concatenate([kx[g, prevs[bi][0], blk_rows(prevs[bi][1]), :], kx[g, cur, blk_rows(bi), :]], axis=0)
             for bi in blocks}
        v = {bi: jnp.concatenate([vx[g, prevs[bi][0], blk_rows(prevs[bi][1]), :], vx[g, cur, blk_rows(bi), :]], axis=0)
             for bi in blocks}
        s = {bi: lax.dot_general(q[bi], k[bi], (((1,), (1,)), ((), ())), preferred_element_type=F32) + bias_ref[g]
             for bi in blocks}
        in_prev = lax.broadcasted_iota(jnp.int32, (Q_BLK, 2 * Q_BLK), 1) < Q_BLK
        s = {bi: s[bi] if prevs[bi][2] is None else jnp.where(jnp.logical_and(prevs[bi][2], in_prev), NEG, s[bi])
             for bi in blocks}
        m = {bi: jnp.max(s[bi], axis=-1, keepdims=True) for bi in blocks}
        p = {bi: jnp.exp(s[bi] - m[bi]) for bi in blocks}
        l = {bi: jnp.sum(p[bi], axis=-1, keepdims=True) for bi in blocks}
        o = {bi: jnp.dot(p[bi].astype(BF16), v[bi], preferred_element_type=F32) / l[bi] for bi in blocks}
        for bi in blocks:
            o_scr[g, token_rows(g, bi), :] = o[bi]
            l_scr[g, token_rows(g, bi), :] = jnp.broadcast_to(m[bi] + jnp.log(l[bi]), (Q_BLK, HEAD_DIM))

    for g in range(N_GROUPS):
        for first in range(0, n_blk, BLOCKS_IN_FLIGHT):
            band_blocks(g, range(first, first + BLOCKS_IN_FLIGHT))

    for c in range(n_blk):
        rows_ = blk_rows(c)
        l0, l1, l2 = l_scr[0, rows_, :], l_scr[1, rows_, :], l_scr[2, rows_, :]
        mx = jnp.maximum(jnp.maximum(l0, l1), l2)
        e0, e1, e2 = jnp.exp(l0 - mx), jnp.exp(l1 - mx), jnp.exp(l2 - mx)
        merged = (e0 * o_scr[0, rows_, :] + e1 * o_scr[1, rows_, :] + e2 * o_scr[2, rows_, :]) / (e0 + e1 + e2)
        gate = gate_ref[rows_, :]
        u_ref[rows_, :] = gate * jax.nn.sigmoid(gate) * merged


def _attn_prompt(proj, bias_band, batch, seq):
    n_span = seq // SPAN
    assert seq % SPAN == 0

    def col_spec(first_col_block):
        return pl.BlockSpec((SPAN, HEAD_DIM),
                            lambda b, h, s, c=first_col_block: (b * n_span + s, c + h))

    head_blocks = QKV_WIDTH // HEAD_DIM
    in_specs = (
        [col_spec(g * HEADS_PER_GROUP) for g in range(N_GROUPS)]
        + [col_spec(head_blocks + g * HEADS_PER_GROUP) for g in range(N_GROUPS)]
        + [col_spec(2 * head_blocks + g * HEADS_PER_GROUP) for g in range(N_GROUPS)]
        + [col_spec(3 * head_blocks)]
        + [pl.BlockSpec((None, N_GROUPS, Q_BLK, 2 * Q_BLK), lambda b, h, s: (h, 0, 0, 0))]
    )
    return pl.pallas_call(
        _attn_prompt_kernel,
        out_shape=jax.ShapeDtypeStruct((batch * seq, ATTN_WIDTH), F32),
        grid=(batch, HEADS_PER_GROUP, n_span),
        in_specs=in_specs,
        out_specs=pl.BlockSpec((SPAN, HEAD_DIM), lambda b, h, s: (b * n_span + s, h)),
        scratch_shapes=[
            pltpu.VMEM((N_GROUPS, 2, SPAN, HEAD_DIM), BF16),
            pltpu.VMEM((N_GROUPS, 2, SPAN, HEAD_DIM), BF16),
            pltpu.VMEM((N_GROUPS, SPAN, HEAD_DIM), F32),
            pltpu.VMEM((N_GROUPS, SPAN, HEAD_DIM), F32),
        ],
        compiler_params=_params(("parallel", "parallel", "arbitrary")),
        name="attn_prompt",
    )(*([proj] * 10), bias_band)


def _decode_kernel(*refs, layer, hosted, stale_other):
    n_h = len(hosted)
    (q_ref, kn_ref, vn_ref, gate_ref, bias_ref, c0_ref, c1_ref, c2_ref), refs = refs[:8], refs[8:]
    cache_hbm, refs = refs[:n_h], refs[n_h:]
    other_new, refs = (refs[:0], refs) if stale_other else (refs[:n_h], refs[n_h:])
    (u_ref, o0_ref, o1_ref, o2_ref), refs = refs[:4], refs[4:]
    out_hbm, (s_scr, buf, rows_scr, in_sems, out_sems, row_sems) = refs[:n_h], refs[n_h:]
    cache_refs = (c0_ref, c1_ref, c2_ref)
    new_refs = (o0_ref, o1_ref, o2_ref)
    n_t = q_ref.shape[0]
    b = pl.program_id(0)

    for g, o_ref in enumerate(new_refs):
        heads = slice(g * HEADS_PER_GROUP, (g + 1) * HEADS_PER_GROUP)
        o_ref[:, 0] = kn_ref[:, heads, :]
        o_ref[:, 1] = vn_ref[:, heads, :]

    n_slots, chunk_rows = buf.shape[:2]
    chunks = []
    for l in range(2):
        for hi, g in enumerate(hosted):
            kept = WINDOWS[g] - n_t
            spans = [(n_t + r0, r0, min(chunk_rows, kept - r0)) for r0 in range(0, kept, chunk_rows)]
            if stale_other and l != layer:
                spans.append((kept, kept, n_t))
            for src0, dst0, n in spans:
                slot = len(chunks) % n_slots
                chunks.append((
                    pltpu.make_async_copy(cache_hbm[hi].at[l, b, pl.ds(src0, n)], buf.at[slot, pl.ds(0, n)],
                                          in_sems.at[slot]),
                    pltpu.make_async_copy(buf.at[slot, pl.ds(0, n)], out_hbm[hi].at[l, b, pl.ds(dst0, n)],
                                          out_sems.at[slot])))
    row_copies = []
    for hi, g in enumerate(hosted):
        newest = pl.ds(WINDOWS[g] - n_t, n_t)
        rows_scr[2 * hi] = new_refs[g][...]
        row_copies.append(pltpu.make_async_copy(rows_scr.at[2 * hi], out_hbm[hi].at[layer, b, newest],
                                                row_sems.at[2 * hi]))
        if not stale_other:
            rows_scr[2 * hi + 1] = other_new[hi][...]
            row_copies.append(pltpu.make_async_copy(rows_scr.at[2 * hi + 1], out_hbm[hi].at[1 - layer, b, newest],
                                                    row_sems.at[2 * hi + 1]))
    for cp_in, _ in chunks[:n_slots]:
        cp_in.start()
    for cp in row_copies:
        cp.start()
    pumped = [0]

    all_resident = len(chunks) <= n_slots

    def pump():
        i = pumped[0]
        if all_resident or i == len(chunks):
            return
        chunks[i][0].wait()
        chunks[i][1].start()
        if i:
            chunks[i - 1][1].wait()
            if i - 1 + n_slots < len(chunks):
                chunks[i - 1 + n_slots][0].start()
        pumped[0] = i + 1

    outs, lses = [], []
    for g in range(N_GROUPS):
        d = DILATIONS[g]
        heads = slice(g * HEADS_PER_GROUP, (g + 1) * HEADS_PER_GROUP)
        c_ref = cache_refs[g]
        n_cache = N_KEYS - (n_t - 1) // d
        q_t = [q_ref[t, heads, :] for t in range(n_t)]

        def key_tile(kv, t, i, g=g, d=d, c_ref=c_ref, heads=heads):
            row = t + i * d
            if row < WINDOWS[g]:
                return c_ref[row // d, row % d, kv]
            return (kn_ref, vn_ref)[kv][row - WINDOWS[g], heads, :]

        def store_scores(keys, tiles, g=g):
            prod = jnp.concatenate([q_t[t] * tiles[ii][t] for ii in range(len(keys)) for t in range(n_t)], axis=0)
            sums = jnp.dot(prod.astype(BF16), jnp.ones((HEAD_DIM, HEAD_DIM), BF16), preferred_element_type=F32)
            for ii, i in enumerate(keys):
                for t in range(n_t):
                    row0 = (ii * n_t + t) * HEADS_PER_GROUP
                    s_scr[t, i] = sums[row0:row0 + HEADS_PER_GROUP] + bias_ref[g, i]

        def scores_body(it, carry, d=d, c_ref=c_ref, store_scores=store_scores):
            keys = [it * SCORE_KEYS + ii for ii in range(SCORE_KEYS)]
            store_scores(keys, [[c_ref[i + t // d, t % d, 0] for t in range(n_t)] for i in keys])
            return carry
        lax.fori_loop(0, n_cache // SCORE_KEYS, scores_body, 0, unroll=SCORE_UNROLL)
        pump()
        rest = list(range(n_cache // SCORE_KEYS * SCORE_KEYS, N_KEYS + 1))
        store_scores(rest, [[key_tile(0, t, i) for t in range(n_t)] for i in rest])
        pump()

        m_t, l_t = [], []
        for t in range(n_t):
            sv = s_scr[t, 0:N_KEYS + 1]
            m = jnp.max(sv, axis=0)
            p = jnp.exp(sv - m)
            s_scr[t, 0:N_KEYS + 1] = p
            m_t.append(m)
            l_t.append(jnp.sum(p, axis=0))
            if t % 2:
                pump()

        def pv_body(i, accs, d=d, c_ref=c_ref):
            return tuple(accs[t] + s_scr[t, i] * c_ref[i + t // d, t % d, 1] for t in range(n_t))
        accs = lax.fori_loop(0, n_cache, pv_body,
                             tuple(jnp.zeros((HEADS_PER_GROUP, HEAD_DIM), F32) for _ in range(n_t)))
        pump()
        accs = list(accs)
        for i in range(n_cache, N_KEYS + 1):
            for t in range(n_t):
                accs[t] = accs[t] + s_scr[t, i] * key_tile(1, t, i)
        outs.append([accs[t] / l_t[t] for t in range(n_t)])
        lses.append([m_t[t] + jnp.log(l_t[t]) for t in range(n_t)])
        if all_resident and g == N_GROUPS - 2:
            for cp_in, cp_out in chunks:
                cp_in.wait()
                cp_out.start()

    for t in range(n_t):
        l0, l1, l2 = lses[0][t], lses[1][t], lses[2][t]
        mx = jnp.maximum(jnp.maximum(l0, l1), l2)
        e0, e1, e2 = jnp.exp(l0 - mx), jnp.exp(l1 - mx), jnp.exp(l2 - mx)
        merged = (e0 * outs[0][t] + e1 * outs[1][t] + e2 * outs[2][t]) / (e0 + e1 + e2)
        gate = gate_ref[t]
        u_ref[t] = gate * jax.nn.sigmoid(gate) * merged

    if all_resident:
        for _, cp_out in chunks:
            cp_out.wait()
    else:
        while pumped[0] < len(chunks):
            pump()
        chunks[-1][1].wait()
    for cp in row_copies:
        cp.wait()


def _decode_attn(layer, proj, bias_dec, caches, hosted, other_new=None):
    bd, n_t = proj.shape[:2]
    assert n_t == SUBLANES and all(c.shape[0] == 2 for c in caches)
    stale_other = other_new is None
    views, view_specs = [], []
    for g, cache in enumerate(caches):
        d = DILATIONS[g]
        n_res = min(d, n_t)
        views.append(cache.reshape(cache.shape[0], bd, N_KEYS, d, 2, HEADS_PER_GROUP, HEAD_DIM))
        view_specs.append(pl.BlockSpec((None, None, N_KEYS, n_res, 2, HEADS_PER_GROUP, HEAD_DIM),
                                       lambda b: (layer, b, 0, 0, 0, 0, 0)))
    qkv_specs = [pl.BlockSpec((None, n_t, N_HEADS, HEAD_DIM), lambda b, i=i: (b, 0, i, 0)) for i in range(3)]
    gate_spec = pl.BlockSpec((None, n_t, HEADS_PER_GROUP, HEAD_DIM),
                             lambda b: (b, 0, 3 * N_HEADS // HEADS_PER_GROUP, 0))
    u_spec = pl.BlockSpec((None, n_t, HEADS_PER_GROUP, HEAD_DIM), lambda b: (b, 0, 0, 0))
    new_rows_shape = (bd, n_t, 2, HEADS_PER_GROUP, HEAD_DIM)
    new_rows_spec = pl.BlockSpec((None,) + new_rows_shape[1:], lambda b: (b, 0, 0, 0, 0))
    any_spec = pl.BlockSpec(memory_space=pl.ANY)
    hosted_caches = [caches[g] for g in hosted]
    others = [] if stale_other else [other_new[g] for g in hosted]
    res = pl.pallas_call(
        functools.partial(_decode_kernel, layer=layer, hosted=tuple(hosted), stale_other=stale_other),
        out_shape=[jax.ShapeDtypeStruct((bd, n_t, HEADS_PER_GROUP, HEAD_DIM), F32)]
        + [jax.ShapeDtypeStruct(new_rows_shape, c.dtype) for c in caches]
        + [jax.ShapeDtypeStruct(c.shape, c.dtype) for c in hosted_caches],
        grid=(bd,),
        in_specs=qkv_specs + [gate_spec, pl.BlockSpec(bias_dec.shape, lambda b: (0, 0, 0, 0))] + view_specs
        + [any_spec] * len(hosted) + [new_rows_spec] * len(others),
        out_specs=[u_spec] + [new_rows_spec] * N_GROUPS + [any_spec] * len(hosted),
        scratch_shapes=[
            pltpu.VMEM((n_t, N_KEYS + SUBLANES, HEADS_PER_GROUP, HEAD_DIM), F32),
            pltpu.VMEM((COPY_SLOTS, COPY_CHUNK_ROWS, 2, HEADS_PER_GROUP, HEAD_DIM), F32),
            pltpu.VMEM((2 * len(hosted), n_t, 2, HEADS_PER_GROUP, HEAD_DIM), F32),
            pltpu.SemaphoreType.DMA((COPY_SLOTS,)),
            pltpu.SemaphoreType.DMA((COPY_SLOTS,)),
            pltpu.SemaphoreType.DMA((2 * len(hosted),)),
        ],
        compiler_params=_params(("parallel",)),
        name="decode_attn",
    )(proj, proj, proj, proj, bias_dec, *views, *hosted_caches, *others)
    return res[0], tuple(res[1:1 + N_GROUPS]), tuple(res[1 + N_GROUPS:])


def _kv_tail_kernel(p0_ref, p1_ref, o_ref):
    layer = pl.program_id(0)
    for p_ref, which in ((p0_ref, 0), (p1_ref, 1)):
        @pl.when(layer == which)
        def _(p_ref=p_ref):
            for h in range(HEADS_PER_GROUP):
                o_ref[:, h, :] = p_ref[:, h * HEAD_DIM:(h + 1) * HEAD_DIM]


def _kv_tails(projs, g, batch, seq):
    d = DILATIONS[g]
    keep = min(WINDOWS[g], seq)
    per_res = keep // d
    assert len(projs) == 2 and keep <= SPAN and keep % d == 0 and (SPAN // d) % per_res == 0
    assert seq % per_res == 0 and per_res % SUBLANES == 0

    def in_rows(layer, b, kv, r):
        first = (b + 1) * seq - SPAN + r * (SPAN // d) + (SPAN - keep) // d
        return first // per_res, (1 + kv) * (QKV_WIDTH // ATTN_WIDTH) + g

    out = pl.pallas_call(
        _kv_tail_kernel,
        out_shape=jax.ShapeDtypeStruct((len(projs), batch, per_res, d, 2, HEADS_PER_GROUP, HEAD_DIM), F32),
        grid=(len(projs), batch, 2, d),
        in_specs=[pl.BlockSpec((per_res, ATTN_WIDTH), in_rows)] * 2,
        out_specs=pl.BlockSpec((None, None, per_res, None, None, HEADS_PER_GROUP, HEAD_DIM),
                               lambda layer, b, kv, r: (layer, b, 0, r, kv, 0, 0)),
        compiler_params=_params(("parallel",) * 4),
        name="kv_tails",
    )(*projs)
    return out.reshape(len(projs), batch, keep, 2, HEADS_PER_GROUP, HEAD_DIM)


def _set_rows_kernel(new_ref, cache_ref, o_ref):
    del cache_ref
    o_ref[...] = new_ref[...]


def _cache_set_rows(cache, new_rows, layer):
    batch, n_new = new_rows.shape[:2]
    tile = new_rows.shape[2:]
    zeros = (0,) * len(tile)
    last_block = cache.shape[2] // n_new - 1
    assert cache.shape[2] % n_new == 0
    return pl.pallas_call(
        _set_rows_kernel,
        out_shape=jax.ShapeDtypeStruct(cache.shape, cache.dtype),
        grid=(batch,),
        in_specs=[pl.BlockSpec((None, n_new) + tile, lambda b: (b, 0) + zeros),
                  pl.BlockSpec(memory_space=pl.ANY)],
        out_specs=pl.BlockSpec((None, None, n_new) + tile, lambda b: (layer, b, last_block) + zeros),
        input_output_aliases={1: 0},
        compiler_params=_params(("parallel",)),
        name="cache_set_rows",
    )(new_rows, cache)


def _scan_rows(a, b, h_prev):
    n_tiles = a.shape[0] // SUBLANES
    a3 = a.reshape(n_tiles, SUBLANES, a.shape[1])
    b3 = b.reshape(n_tiles, SUBLANES, b.shape[1])
    row = lax.broadcasted_iota(jnp.int32, a3.shape, 1)
    shift = 1
    while shift < SUBLANES:
        a_sh = pltpu.roll(a3, shift, axis=1)
        b_sh = pltpu.roll(b3, shift, axis=1)
        keep = row >= shift
        b3 = jnp.where(keep, a3 * b_sh + b3, b3)
        a3 = jnp.where(keep, a3 * a_sh, a3)
        shift *= 2
    tiles = []
    carry = h_prev
    for k in range(n_tiles):
        h_k = b3[k] + a3[k] * carry
        carry = h_k[SUBLANES - 1:SUBLANES]
        tiles.append(h_k)
    return jnp.concatenate(tiles, axis=0), carry


def _rnn_kernel(x_ref, ng_ref, win_ref, conv0_ref, h0_ref, cw_ref, cb_ref, gaw_ref, gab_ref,
                gxw_ref, gxb_ref, lam_ref, wout_ref, y_ref, hlast_ref, clast_ref,
                xe_scr, gate_scr, h_scr, u_scr):
    c = pl.program_id(1)
    tl = x_ref.shape[0]
    pad = SUBLANES

    @pl.when(c == 0)
    def _():
        xe_scr[pad - (CONV_W - 1):pad] = conv0_ref[...]
        h_scr[...] = h0_ref[...]

    @pl.when(c > 0)
    def _():
        xe_scr[0:pad] = xe_scr[tl:tl + pad]

    x = x_ref[...]
    ms = jnp.mean(x * x, axis=-1, keepdims=True)
    hn = (x * lax.rsqrt(ms + EPS) * ng_ref[...]).astype(BF16)
    xe_scr[pad:pad + tl] = jnp.dot(hn, win_ref[:, :D_RNN], preferred_element_type=F32)
    gate_scr[...] = jnp.dot(hn, win_ref[:, D_RNN:], preferred_element_type=F32)
    clast_ref[...] = xe_scr[pad + tl - (CONV_W - 1):pad + tl]

    for j in range(RNN_BLOCKS):
        cols = slice(j * RNN_BLOCK_W, (j + 1) * RNN_BLOCK_W)
        xc = cb_ref[:, cols]
        for k in range(CONV_W):
            start = pad - (CONV_W - 1) + k
            xc = xc + cw_ref[k:k + 1, cols] * xe_scr[start:start + tl, cols]
        xcb = xc.astype(BF16)
        r = jax.nn.sigmoid(jnp.dot(xcb, gaw_ref[j], preferred_element_type=F32) + gab_ref[:, cols])
        ig = jax.nn.sigmoid(jnp.dot(xcb, gxw_ref[j], preferred_element_type=F32) + gxb_ref[:, cols])
        neg_lam = -lam_ref[:, cols]
        softplus = jnp.maximum(neg_lam, 0.0) + jnp.log1p(jnp.exp(-jnp.abs(neg_lam)))
        log_a = -LRU_C * r * softplus
        a = jnp.exp(log_a)
        bterm = jnp.sqrt(-jnp.tanh(log_a) * (a * a + 1.0)) * (ig * xc)
        h, h_end = _scan_rows(a, bterm, h_scr[:, cols])
        h_scr[:, cols] = h_end
        gate = gate_scr[:, cols]
        u_scr[:, cols] = gate * jax.nn.sigmoid(gate) * h

    hlast_ref[...] = h_scr[...]
    y_ref[...] = x_ref[...] + jnp.dot(u_scr[...].astype(BF16), wout_ref[...], preferred_element_type=F32)


def _rnn_layer(x, norm_g, w_in, conv0, h0, conv_w, conv_b, ga_w, ga_b, gx_w, gx_b, lam, w_out, batch, seq, tl):
    n_chunks = seq // tl
    assert seq % tl == 0 and tl % SUBLANES == 0
    row = lambda b, c: (b * n_chunks + c, 0)
    vec = lambda a: a.reshape(1, -1)
    full = lambda shape: pl.BlockSpec(shape, lambda b, c: (0,) * len(shape))
    per_batch = lambda rows_: pl.BlockSpec((None, rows_, D_RNN), lambda b, c: (b, 0, 0))
    y, h_last, c_last = pl.pallas_call(
        _rnn_kernel,
        out_shape=[jax.ShapeDtypeStruct((batch * seq, D_MODEL), F32),
                   jax.ShapeDtypeStruct((batch, 1, D_RNN), F32),
                   jax.ShapeDtypeStruct((batch, CONV_W - 1, D_RNN), F32)],
        grid=(batch, n_chunks),
        in_specs=[
            pl.BlockSpec((tl, D_MODEL), row),
            full((1, D_MODEL)),
            full((D_MODEL, 2 * D_RNN)),
            per_batch(CONV_W - 1),
            per_batch(1),
            full((CONV_W, D_RNN)),
            full((1, D_RNN)),
            full((RNN_BLOCKS, RNN_BLOCK_W, RNN_BLOCK_W)),
            full((1, D_RNN)),
            full((RNN_BLOCKS, RNN_BLOCK_W, RNN_BLOCK_W)),
            full((1, D_RNN)),
            full((1, D_RNN)),
            full((D_RNN, D_MODEL)),
        ],
        out_specs=[pl.BlockSpec((tl, D_MODEL), row), per_batch(1), per_batch(CONV_W - 1)],
        scratch_shapes=[
            pltpu.VMEM((tl + SUBLANES, D_RNN), F32),
            pltpu.VMEM((tl, D_RNN), F32),
            pltpu.VMEM((1, D_RNN), F32),
            pltpu.VMEM((tl, D_RNN), F32),
        ],
        compiler_params=_params(("parallel", "arbitrary")),
        name="rnn_layer",
    )(x, vec(norm_g), w_in, conv0, h0.reshape(batch, 1, D_RNN), conv_w, vec(conv_b), ga_w, vec(ga_b),
      gx_w, vec(gx_b), vec(lam), w_out)
    return y, h_last.reshape(batch, D_RNN), c_last


def _t5_bucket(dist):
    n = jnp.maximum(dist, 0)
    max_exact = N_BUCKETS // 2
    nf = jnp.maximum(n, 1).astype(F32)
    large = max_exact + (jnp.log(nf / max_exact) / math.log(MAX_DISTANCE / max_exact)
                         * (N_BUCKETS - max_exact)).astype(jnp.int32)
    large = jnp.minimum(large, N_BUCKETS - 1)
    return jnp.where(n < max_exact, n, large)


def _bias_tables(rel_bias):
    steps = jnp.arange(N_KEYS + 1)
    period = 3 * Q_BLK
    bands, decs = [], []
    for g in range(N_GROUPS):
        heads = slice(g * HEADS_PER_GROUP, (g + 1) * HEADS_PER_GROUP)
        per_step = rel_bias[_t5_bucket(steps * DILATIONS[g])][:, heads]
        row = jnp.concatenate([per_step[::-1].T,
                               jnp.full((HEADS_PER_GROUP, period - (N_KEYS + 1)), NEG, F32)], axis=1)
        band = jnp.tile(row, (1, Q_BLK))[:, :Q_BLK * (period - 1)]
        bands.append(band.reshape(HEADS_PER_GROUP, Q_BLK, period - 1)[:, :, :2 * Q_BLK])
        decs.append(jnp.broadcast_to(per_step[::-1][:, :, None],
                                     (N_KEYS + 1, HEADS_PER_GROUP, HEAD_DIM)))
    return jnp.stack(bands, axis=1).astype(F32), jnp.stack(decs).astype(F32)


def kernel(x_prompt, x_sample, cache_kv_w128, cache_kv_w512, cache_kv_w2048, state_rglru_h,
           state_rglru_conv, attn_norm, attn_w_in, attn_q_norm, attn_k_norm, attn_w_out, rel_bias,
           rnn_norm, rnn_w_in, rnn_conv_w, rnn_conv_b, rnn_gate_a_w, rnn_gate_a_b, rnn_gate_x_w,
           rnn_gate_x_b, rnn_lambda, rnn_w_out):
    batch, seq, _ = x_prompt.shape
    bd, n_t, _ = x_sample.shape
    depth = attn_norm.shape[0] + rnn_norm.shape[0]
    caches = (cache_kv_w128, cache_kv_w512, cache_kv_w2048)

    yp = x_prompt.reshape(batch * seq, D_MODEL)
    ys = x_sample.reshape(bd * n_t, D_MODEL)
    tm_p, tm_s = OUT_ROWS, bd * n_t

    bias_band, bias_dec = _bias_tables(rel_bias)
    projs_p = []
    assert depth == 4
    h_p, h_s, c_p, c_s = [], [], [], []
    for i in range(depth):
        li = i // 2
        if i % 2 == 0:
            w_in = attn_w_in[li].astype(BF16)
            w_out = attn_w_out[li].astype(BF16)
            head_scale = jnp.concatenate([
                jnp.tile(attn_q_norm[li] * (HEAD_DIM ** -0.5), N_HEADS),
                jnp.tile(attn_k_norm[li], N_HEADS)]).reshape(1, 2 * QKV_WIDTH)
            proj_p = _norm_proj(yp, attn_norm[li], w_in, head_scale, SPAN, PROJ_COLS, row_orders=DILATIONS)
            proj_s = _norm_proj(ys, attn_norm[li], w_in, head_scale, tm_s, PROJ_COLS, split_heads=True)
            projs_p.append(proj_p)

            u_p = _attn_prompt(proj_p, bias_band, batch, seq)
            yp = _out_proj(u_p, w_out, yp, tm_p)

            proj_s = proj_s.reshape(bd, n_t, -1, HEAD_DIM)
            if li == 0:
                u_s, new_rows0, (cache_w128, cache_w512) = _decode_attn(li, proj_s, bias_dec, caches, (0, 1))
            else:
                u_s, new_rows1, (cache_w2048,) = _decode_attn(li, proj_s, bias_dec, caches, (2,), new_rows0)
                cache_w128 = _cache_set_rows(cache_w128, new_rows1[0], li)
                cache_w512 = _cache_set_rows(cache_w512, new_rows1[1], li)
            ys = _out_proj(u_s.reshape(bd * n_t, ATTN_WIDTH), w_out, ys, tm_s)
        else:
            w_in = rnn_w_in[li].astype(BF16)
            rnn_args = (rnn_conv_w[li], rnn_conv_b[li], rnn_gate_a_w[li].astype(BF16), rnn_gate_a_b[li].reshape(-1),
                        rnn_gate_x_w[li].astype(BF16), rnn_gate_x_b[li].reshape(-1), rnn_lambda[li],
                        rnn_w_out[li].astype(BF16))
            yp, hp, cp = _rnn_layer(yp, rnn_norm[li], w_in, jnp.zeros((batch, CONV_W - 1, D_RNN), F32),
                                    jnp.zeros((batch, D_RNN), F32), *rnn_args, batch, seq, RNN_CHUNK)
            ys, hs, cs = _rnn_layer(ys, rnn_norm[li], w_in, state_rglru_conv[li], state_rglru_h[li], *rnn_args,
                                    bd, n_t, n_t)
            h_p.append(hp)
            h_s.append(hs)
            c_p.append(cp)
            c_s.append(cs)

    kv_p = [_kv_tails(projs_p, g, batch, seq) for g in range(N_GROUPS)]
    return (yp.reshape(batch, seq, D_MODEL), ys.reshape(bd, n_t, D_MODEL),
            kv_p[0], cache_w128,
            kv_p[1], cache_w512,
            kv_p[2], cache_w2048,
            jnp.stack(h_p), jnp.stack(h_s),
            jnp.stack(c_p), jnp.stack(c_s))
```

```python
import functools
import math

import jax
import jax.numpy as jnp
from jax import lax
from jax.experimental import pallas as pl
from jax.experimental.pallas import tpu as pltpu

F32 = jnp.float32
BF16 = jnp.bfloat16

D_MODEL = 1024
N_GROUPS = 3
WINDOWS = (128, 512, 2048)
DILATIONS = (1, 4, 16)
HEADS_PER_GROUP = 8
HEAD_DIM = 128
N_HEADS = N_GROUPS * HEADS_PER_GROUP
QKV_WIDTH = N_HEADS * HEAD_DIM
ATTN_WIDTH = HEADS_PER_GROUP * HEAD_DIM
Q_BLK = 128
N_KEYS = Q_BLK
SPAN = Q_BLK * DILATIONS[-1]
BLOCKS_IN_FLIGHT = 8
PROJ_COLS = 1024
OUT_ROWS = 1024
RNN_CHUNK = 256
COPY_CHUNK_ROWS = 256
COPY_SLOTS = 8
SCORE_KEYS = 8
SCORE_UNROLL = 4
N_BUCKETS = 32
MAX_DISTANCE = 2048
D_RNN = 1280
RNN_BLOCKS = 10
RNN_BLOCK_W = D_RNN // RNN_BLOCKS
CONV_W = 4
LRU_C = 8.0
EPS = 1e-6
NEG = -1e30

SUBLANES = 8
MXU_COLS = 256
VMEM_LIMIT = 52 * 1024 * 1024


def _params(semantics):
    return pltpu.CompilerParams(dimension_semantics=semantics, vmem_limit_bytes=VMEM_LIMIT)


def _residue_rows(start, size, stride):
    return pl.ds(start, size) if stride == 1 else pl.ds(start, size, stride=stride)


def _proj_kernel(*refs, n_norm_tiles, row_orders, tiles_per_order, n_slabs):
    x_refs, (g_ref, w_ref), rest = refs[:n_slabs], refs[n_slabs:n_slabs + 2], refs[n_slabs + 2:]
    hs_ref, o_ref, h_scr = rest
    n = pl.program_id(1)
    tm = x_refs[0].shape[0]
    k_dim = n_slabs * HEAD_DIM

    @pl.when(n == 0)
    def _():
        for i, d in enumerate(row_orders):
            for r in range(d):
                xs = [x_ref[_residue_rows(r, tm // d, d), :] for x_ref in x_refs]
                ms = jnp.sum(sum(x * x for x in xs), axis=-1, keepdims=True) * (1.0 / k_dim)
                inv = lax.rsqrt(ms + EPS)
                for c, x in enumerate(xs):
                    cols = slice(c * HEAD_DIM, (c + 1) * HEAD_DIM)
                    h_scr[i, r * (tm // d):(r + 1) * (tm // d), cols] = (x * inv * g_ref[:, cols]).astype(BF16)

    if len(row_orders) == 1:
        h = h_scr[0]
    else:
        order = jnp.where(n < tiles_per_order * len(row_orders) * 3, (n // tiles_per_order) % len(row_orders), 0)
        h = h_scr[order]
    tn = w_ref.shape[1]

    def store(head, value):
        if len(o_ref.shape) == 3:
            o_ref[:, head, :] = value
        else:
            o_ref[:, head * HEAD_DIM:(head + 1) * HEAD_DIM] = value

    @pl.when(n < n_norm_tiles)
    def _():
        for c in range(tn // MXU_COLS):
            acc = jnp.dot(h, w_ref[:, c * MXU_COLS:(c + 1) * MXU_COLS], preferred_element_type=F32)
            for hh in range(MXU_COLS // HEAD_DIM):
                head = c * (MXU_COLS // HEAD_DIM) + hh
                a = acc[:, hh * HEAD_DIM:(hh + 1) * HEAD_DIM]
                ms = jnp.mean(a * a, axis=-1, keepdims=True)
                store(head, a * lax.rsqrt(ms + EPS) * hs_ref[:, head * HEAD_DIM:(head + 1) * HEAD_DIM])

    @pl.when(n >= n_norm_tiles)
    def _():
        acc = jnp.dot(h, w_ref[...], preferred_element_type=F32)
        if len(o_ref.shape) == 3:
            for head in range(tn // HEAD_DIM):
                store(head, acc[:, head * HEAD_DIM:(head + 1) * HEAD_DIM])
        else:
            o_ref[...] = acc


def _norm_proj(x, norm_g, w, head_scale, tm, tn, row_orders=(1,), split_heads=False):
    m_rows, k_dim = x.shape
    n_cols = w.shape[1]
    n_norm_cols = head_scale.shape[1]
    assert m_rows % tm == 0 and n_cols % tn == 0 and n_norm_cols % tn == 0 and tn % MXU_COLS == 0
    assert len(row_orders) == 1 or ATTN_WIDTH % tn == 0
    n_slabs = k_dim // HEAD_DIM
    last_norm_tile = n_norm_cols // tn - 1
    if split_heads:
        out_shape = (m_rows, n_cols // HEAD_DIM, HEAD_DIM)
        out_spec = pl.BlockSpec((tm, tn // HEAD_DIM, HEAD_DIM), lambda m, n: (m, n, 0))
    else:
        out_shape = (m_rows, n_cols)
        out_spec = pl.BlockSpec((tm, tn), lambda m, n: (m, n))
    in_specs = [pl.BlockSpec((tm, HEAD_DIM), lambda m, n, c=c: (m, c), pipeline_mode=pl.Buffered(1))
                for c in range(n_slabs)] + [
        pl.BlockSpec((1, k_dim), lambda m, n: (0, 0)),
        pl.BlockSpec((k_dim, tn), lambda m, n: (0, n)),
        pl.BlockSpec((1, tn), lambda m, n: (0, jnp.minimum(n, last_norm_tile))),
    ]
    args = [x] * n_slabs + [norm_g.reshape(1, k_dim), w, head_scale]
    return pl.pallas_call(
        functools.partial(_proj_kernel, n_norm_tiles=n_norm_cols // tn, row_orders=row_orders,
                          tiles_per_order=ATTN_WIDTH // tn, n_slabs=n_slabs),
        out_shape=jax.ShapeDtypeStruct(out_shape, F32),
        grid=(m_rows // tm, n_cols // tn),
        in_specs=in_specs,
        out_specs=out_spec,
        scratch_shapes=[pltpu.VMEM((len(row_orders), tm, k_dim), BF16)],
        compiler_params=_params(("parallel", "arbitrary")),
        name="norm_proj",
    )(*args)


def _out_kernel(u_ref, w_ref, x_ref, y_ref):
    y_ref[...] = x_ref[...] + jnp.dot(u_ref[...].astype(BF16), w_ref[...], preferred_element_type=F32)


def _out_proj(u, w, x, tm):
    m_rows, k_dim = u.shape
    n_cols = w.shape[1]
    assert m_rows % tm == 0
    return pl.pallas_call(
        _out_kernel,
        out_shape=jax.ShapeDtypeStruct((m_rows, n_cols), F32),
        grid=(m_rows // tm,),
        in_specs=[
            pl.BlockSpec((tm, k_dim), lambda m: (m, 0)),
            pl.BlockSpec((k_dim, n_cols), lambda m: (0, 0)),
            pl.BlockSpec((tm, n_cols), lambda m: (m, 0)),
        ],
        out_specs=pl.BlockSpec((tm, n_cols), lambda m: (m, 0)),
        compiler_params=_params(("parallel",)),
        name="out_proj",
    )(u, w, x)


def _attn_prompt_kernel(q0, q1, q2, k0, k1, k2, v0, v1, v2, gate_ref, bias_ref, u_ref,
                        kx, vx, o_scr, l_scr):
    span = pl.program_id(2)
    q_refs = (q0, q1, q2)
    n_blk = SPAN // Q_BLK
    cur, prev = span % 2, 1 - span % 2

    @pl.when(span == 0)
    def _():
        kx[:, 1] = jnp.zeros((N_GROUPS, SPAN, HEAD_DIM), BF16)
        vx[:, 1] = jnp.zeros((N_GROUPS, SPAN, HEAD_DIM), BF16)

    for g, (k_ref, v_ref) in enumerate(((k0, v0), (k1, v1), (k2, v2))):
        kx[g, cur] = k_ref[...].astype(BF16)
        vx[g, cur] = v_ref[...].astype(BF16)

    first_span = span == 0

    def blk_rows(bi):
        return slice(bi * Q_BLK, (bi + 1) * Q_BLK)

    def prev_block(g, bi):
        per_residue = n_blk // DILATIONS[g]
        if bi % per_residue:
            return cur, bi - 1, None
        return prev, bi + per_residue - 1, first_span

    def token_rows(g, bi):
        d = DILATIONS[g]
        per_residue = n_blk // d
        return _residue_rows((bi % per_residue) * Q_BLK * d + bi // per_residue, Q_BLK, d)

    def band_blocks(g, blocks):
        prevs = {bi: prev_block(g, bi) for bi in blocks}
        q = {bi: q_refs[g][blk_rows(bi), :].astype(BF16) for bi in blocks}
        k = {bi: jnp.concatenate([kx[g, prevs[bi][0], blk_rows(prevs[bi][1]), :], kx[g, cur, blk_rows(bi), :]], axis=0)
             for bi in blocks}
        v = {bi: jnp.concatenate([vx[g, prevs[bi][0], blk_rows(prevs[bi][1]), :], vx[g, cur, blk_rows(bi), :]], axis=0)
             for bi in blocks}
        s = {bi: lax.dot_general(q[bi], k[bi], (((1,), (1,)), ((), ())), preferred_element_type=F32) + bias_ref[g]
             for bi in blocks}
        in_prev = lax.broadcasted_iota(jnp.int32, (Q_BLK, 2 * Q_BLK), 1) < Q_BLK
        s = {bi: s[bi] if prevs[bi][2] is None else jnp.where(jnp.logical_and(prevs[bi][2], in_prev), NEG, s[bi])
             for bi in blocks}
        m = {bi: jnp.max(s[bi], axis=-1, keepdims=True) for bi in blocks}
        p = {bi: jnp.exp(s[bi] - m[bi]) for bi in blocks}
        l = {bi: jnp.sum(p[bi], axis=-1, keepdims=True) for bi in blocks}
        o = {bi: jnp.dot(p[bi].astype(BF16), v[bi], preferred_element_type=F32) / l[bi] for bi in blocks}
        for bi in blocks:
            o_scr[g, token_rows(g, bi), :] = o[bi]
            l_scr[g, token_rows(g, bi), :] = jnp.broadcast_to(m[bi] + jnp.log(l[bi]), (Q_BLK, HEAD_DIM))

    for g in range(N_GROUPS):
        for first in range(0, n_blk, BLOCKS_IN_FLIGHT):
            band_blocks(g, range(first, first + BLOCKS_IN_FLIGHT))

    for c in range(n_blk):
        rows_ = blk_rows(c)
        l0, l1, l2 = l_scr[0, rows_, :], l_scr[1, rows_, :], l_scr[2, rows_, :]
        mx = jnp.maximum(jnp.maximum(l0, l1), l2)
        e0, e1, e2 = jnp.exp(l0 - mx), jnp.exp(l1 - mx), jnp.exp(l2 - mx)
        merged = (e0 * o_scr[0, rows_, :] + e1 * o_scr[1, rows_, :] + e2 * o_scr[2, rows_, :]) / (e0 + e1 + e2)
        gate = gate_ref[rows_, :]
        u_ref[rows_, :] = gate * jax.nn.sigmoid(gate) * merged


def _attn_prompt(proj, bias_band, batch, seq):
    n_span = seq // SPAN
    assert seq % SPAN == 0

    def col_spec(first_col_block):
        return pl.BlockSpec((SPAN, HEAD_DIM),
                            lambda b, h, s, c=first_col_block: (b * n_span + s, c + h))

    head_blocks = QKV_WIDTH // HEAD_DIM
    in_specs = (
        [col_spec(g * HEADS_PER_GROUP) for g in range(N_GROUPS)]
        + [col_spec(head_blocks + g * HEADS_PER_GROUP) for g in range(N_GROUPS)]
        + [col_spec(2 * head_blocks + g * HEADS_PER_GROUP) for g in range(N_GROUPS)]
        + [col_spec(3 * head_blocks)]
        + [pl.BlockSpec((None, N_GROUPS, Q_BLK, 2 * Q_BLK), lambda b, h, s: (h, 0, 0, 0))]
    )
    return pl.pallas_call(
        _attn_prompt_kernel,
        out_shape=jax.ShapeDtypeStruct((batch * seq, ATTN_WIDTH), F32),
        grid=(batch, HEADS_PER_GROUP, n_span),
        in_specs=in_specs,
        out_specs=pl.BlockSpec((SPAN, HEAD_DIM), lambda b, h, s: (b * n_span + s, h)),
        scratch_shapes=[
            pltpu.VMEM((N_GROUPS, 2, SPAN, HEAD_DIM), BF16),
            pltpu.VMEM((N_GROUPS, 2, SPAN, HEAD_DIM), BF16),
            pltpu.VMEM((N_GROUPS, SPAN, HEAD_DIM), F32),
            pltpu.VMEM((N_GROUPS, SPAN, HEAD_DIM), F32),
        ],
        compiler_params=_params(("parallel", "parallel", "arbitrary")),
        name="attn_prompt",
    )(*([proj] * 10), bias_band)


def _shift_copy_chunks(cache_hbm, out_hbm, layer, b, window, n_new, buf, in_sems, out_sems, first_slot, keep_newest):
    n_slots, chunk_rows = buf.shape[:2]
    kept = window - n_new
    spans = [(n_new + r0, r0, min(chunk_rows, kept - r0)) for r0 in range(0, kept, chunk_rows)]
    if keep_newest:
        spans.append((kept, kept, n_new))
    pairs = []
    for i, (src0, dst0, n) in enumerate(spans):
        slot = (first_slot + i) % n_slots
        pairs.append((
            pltpu.make_async_copy(cache_hbm.at[layer, b, pl.ds(src0, n)], buf.at[slot, pl.ds(0, n)], in_sems.at[slot]),
            pltpu.make_async_copy(buf.at[slot, pl.ds(0, n)], out_hbm.at[layer, b, pl.ds(dst0, n)], out_sems.at[slot])))
    return pairs


def _decode_kernel(*refs, layer, hosted, stale_other):
    n_h = len(hosted)
    (q_ref, kn_ref, vn_ref, gate_ref, bias_ref, c0_ref, c1_ref, c2_ref), refs = refs[:8], refs[8:]
    cache_hbm, refs = refs[:n_h], refs[n_h:]
    other_new, refs = (refs[:0], refs) if stale_other else (refs[:n_h], refs[n_h:])
    (u_ref, o0_ref, o1_ref, o2_ref), refs = refs[:4], refs[4:]
    out_hbm, (s_scr, buf, rows_scr, in_sems, out_sems, row_sems) = refs[:n_h], refs[n_h:]
    cache_refs = (c0_ref, c1_ref, c2_ref)
    new_refs = (o0_ref, o1_ref, o2_ref)
    n_t = q_ref.shape[0]
    b = pl.program_id(0)

    for g, o_ref in enumerate(new_refs):
        heads = slice(g * HEADS_PER_GROUP, (g + 1) * HEADS_PER_GROUP)
        o_ref[:, 0] = kn_ref[:, heads, :]
        o_ref[:, 1] = vn_ref[:, heads, :]

    n_slots = buf.shape[0]
    chunks = []
    for l in range(2):
        for hi, g in enumerate(hosted):
            chunks += _shift_copy_chunks(cache_hbm[hi], out_hbm[hi], l, b, WINDOWS[g], n_t, buf, in_sems, out_sems,
                                         first_slot=len(chunks), keep_newest=stale_other and l != layer)
    row_copies = []
    for hi, g in enumerate(hosted):
        newest = pl.ds(WINDOWS[g] - n_t, n_t)
        rows_scr[2 * hi] = new_refs[g][...]
        row_copies.append(pltpu.make_async_copy(rows_scr.at[2 * hi], out_hbm[hi].at[layer, b, newest],
                                                row_sems.at[2 * hi]))
        if not stale_other:
            rows_scr[2 * hi + 1] = other_new[hi][...]
            row_copies.append(pltpu.make_async_copy(rows_scr.at[2 * hi + 1], out_hbm[hi].at[1 - layer, b, newest],
                                                    row_sems.at[2 * hi + 1]))
    for cp_in, _ in chunks[:n_slots]:
        cp_in.start()
    for cp in row_copies:
        cp.start()
    pumped = [0]

    all_resident = len(chunks) <= n_slots

    def pump():
        i = pumped[0]
        if all_resident or i == len(chunks):
            return
        chunks[i][0].wait()
        chunks[i][1].start()
        if i:
            chunks[i - 1][1].wait()
            if i - 1 + n_slots < len(chunks):
                chunks[i - 1 + n_slots][0].start()
        pumped[0] = i + 1

    outs, lses = [], []
    for g in range(N_GROUPS):
        d = DILATIONS[g]
        heads = slice(g * HEADS_PER_GROUP, (g + 1) * HEADS_PER_GROUP)
        c_ref = cache_refs[g]
        n_cache = N_KEYS - (n_t - 1) // d
        q_t = [q_ref[t, heads, :] for t in range(n_t)]

        def key_tile(kv, t, i, g=g, d=d, c_ref=c_ref, heads=heads):
            row = t + i * d
            if row < WINDOWS[g]:
                return c_ref[row // d, row % d, kv]
            return (kn_ref, vn_ref)[kv][row - WINDOWS[g], heads, :]

        def store_scores(keys, tiles, g=g):
            prod = jnp.concatenate([q_t[t] * tiles[ii][t] for ii in range(len(keys)) for t in range(n_t)], axis=0)
            sums = jnp.dot(prod.astype(BF16), jnp.ones((HEAD_DIM, HEAD_DIM), BF16), preferred_element_type=F32)
            for ii, i in enumerate(keys):
                for t in range(n_t):
                    row0 = (ii * n_t + t) * HEADS_PER_GROUP
                    s_scr[t, i] = sums[row0:row0 + HEADS_PER_GROUP] + bias_ref[g, i]

        def scores_body(it, carry, d=d, c_ref=c_ref, store_scores=store_scores):
            keys = [it * SCORE_KEYS + ii for ii in range(SCORE_KEYS)]
            store_scores(keys, [[c_ref[i + t // d, t % d, 0] for t in range(n_t)] for i in keys])
            return carry
        lax.fori_loop(0, n_cache // SCORE_KEYS, scores_body, 0, unroll=SCORE_UNROLL)
        pump()
        rest = list(range(n_cache // SCORE_KEYS * SCORE_KEYS, N_KEYS + 1))
        store_scores(rest, [[key_tile(0, t, i) for t in range(n_t)] for i in rest])
        pump()

        m_t, l_t = [], []
        for t in range(n_t):
            sv = s_scr[t, 0:N_KEYS + 1]
            m = jnp.max(sv, axis=0)
            p = jnp.exp(sv - m)
            s_scr[t, 0:N_KEYS + 1] = p
            m_t.append(m)
            l_t.append(jnp.sum(p, axis=0))
            if t % 2:
                pump()

        def pv_body(i, accs, d=d, c_ref=c_ref):
            return tuple(accs[t] + s_scr[t, i] * c_ref[i + t // d, t % d, 1] for t in range(n_t))
        accs = lax.fori_loop(0, n_cache, pv_body,
                             tuple(jnp.zeros((HEADS_PER_GROUP, HEAD_DIM), F32) for _ in range(n_t)))
        pump()
        accs = list(accs)
        for i in range(n_cache, N_KEYS + 1):
            for t in range(n_t):
                accs[t] = accs[t] + s_scr[t, i] * key_tile(1, t, i)
        outs.append([accs[t] / l_t[t] for t in range(n_t)])
        lses.append([m_t[t] + jnp.log(l_t[t]) for t in range(n_t)])
        if all_resident and g == N_GROUPS - 2:
            for cp_in, cp_out in chunks:
                cp_in.wait()
                cp_out.start()

    for t in range(n_t):
        l0, l1, l2 = lses[0][t], lses[1][t], lses[2][t]
        mx = jnp.maximum(jnp.maximum(l0, l1), l2)
        e0, e1, e2 = jnp.exp(l0 - mx), jnp.exp(l1 - mx), jnp.exp(l2 - mx)
        merged = (e0 * outs[0][t] + e1 * outs[1][t] + e2 * outs[2][t]) / (e0 + e1 + e2)
        gate = gate_ref[t]
        u_ref[t] = gate * jax.nn.sigmoid(gate) * merged

    if all_resident:
        for _, cp_out in chunks:
            cp_out.wait()
    else:
        while pumped[0] < len(chunks):
            pump()
        chunks[-1][1].wait()
    for cp in row_copies:
        cp.wait()


def _decode_attn(layer, proj, bias_dec, caches, hosted, other_new=None):
    bd, n_t = proj.shape[:2]
    assert n_t == SUBLANES and all(c.shape[0] == 2 for c in caches)
    stale_other = other_new is None
    views, view_specs = [], []
    for g, cache in enumerate(caches):
        d = DILATIONS[g]
        n_res = min(d, n_t)
        views.append(cache.reshape(cache.shape[0], bd, N_KEYS, d, 2, HEADS_PER_GROUP, HEAD_DIM))
        view_specs.append(pl.BlockSpec((None, None, N_KEYS, n_res, 2, HEADS_PER_GROUP, HEAD_DIM),
                                       lambda b: (layer, b, 0, 0, 0, 0, 0)))
    qkv_specs = [pl.BlockSpec((None, n_t, N_HEADS, HEAD_DIM), lambda b, i=i: (b, 0, i, 0)) for i in range(3)]
    gate_spec = pl.BlockSpec((None, n_t, HEADS_PER_GROUP, HEAD_DIM),
                             lambda b: (b, 0, 3 * N_HEADS // HEADS_PER_GROUP, 0))
    u_spec = pl.BlockSpec((None, n_t, HEADS_PER_GROUP, HEAD_DIM), lambda b: (b, 0, 0, 0))
    new_rows_shape = (bd, n_t, 2, HEADS_PER_GROUP, HEAD_DIM)
    new_rows_spec = pl.BlockSpec((None,) + new_rows_shape[1:], lambda b: (b, 0, 0, 0, 0))
    any_spec = pl.BlockSpec(memory_space=pl.ANY)
    n_slots, chunk_rows = (COPY_SLOTS, COPY_CHUNK_ROWS) if hosted else (1, n_t)
    hosted_caches = [caches[g] for g in hosted]
    others = [] if stale_other else [other_new[g] for g in hosted]
    res = pl.pallas_call(
        functools.partial(_decode_kernel, layer=layer, hosted=tuple(hosted), stale_other=stale_other),
        out_shape=[jax.ShapeDtypeStruct((bd, n_t, HEADS_PER_GROUP, HEAD_DIM), F32)]
        + [jax.ShapeDtypeStruct(new_rows_shape, c.dtype) for c in caches]
        + [jax.ShapeDtypeStruct(c.shape, c.dtype) for c in hosted_caches],
        grid=(bd,),
        in_specs=qkv_specs + [gate_spec, pl.BlockSpec(bias_dec.shape, lambda b: (0, 0, 0, 0))] + view_specs
        + [any_spec] * len(hosted) + [new_rows_spec] * len(others),
        out_specs=[u_spec] + [new_rows_spec] * N_GROUPS + [any_spec] * len(hosted),
        scratch_shapes=[
            pltpu.VMEM((n_t, N_KEYS + SUBLANES, HEADS_PER_GROUP, HEAD_DIM), F32),
            pltpu.VMEM((n_slots, chunk_rows, 2, HEADS_PER_GROUP, HEAD_DIM), F32),
            pltpu.VMEM((max(1, 2 * len(hosted)), n_t, 2, HEADS_PER_GROUP, HEAD_DIM), F32),
            pltpu.SemaphoreType.DMA((n_slots,)),
            pltpu.SemaphoreType.DMA((n_slots,)),
            pltpu.SemaphoreType.DMA((max(1, 2 * len(hosted)),)),
        ],
        compiler_params=_params(("parallel",)),
        name="decode_attn",
    )(proj, proj, proj, proj, bias_dec, *views, *hosted_caches, *others)
    return res[0], tuple(res[1:1 + N_GROUPS]), tuple(res[1 + N_GROUPS:])


def _kv_tail_kernel(p0_ref, p1_ref, o_ref):
    layer = pl.program_id(0)
    for p_ref, which in ((p0_ref, 0), (p1_ref, 1)):
        @pl.when(layer == which)
        def _(p_ref=p_ref):
            for h in range(HEADS_PER_GROUP):
                o_ref[:, h, :] = p_ref[:, h * HEAD_DIM:(h + 1) * HEAD_DIM]


def _kv_tails(projs, g, batch, seq):
    d = DILATIONS[g]
    keep = min(WINDOWS[g], seq)
    per_res = keep // d
    assert len(projs) == 2 and keep <= SPAN and keep % d == 0 and (SPAN // d) % per_res == 0
    assert seq % per_res == 0 and per_res % SUBLANES == 0

    def in_rows(layer, b, kv, r):
        first = (b + 1) * seq - SPAN + r * (SPAN // d) + (SPAN - keep) // d
        return first // per_res, (1 + kv) * (QKV_WIDTH // ATTN_WIDTH) + g

    out = pl.pallas_call(
        _kv_tail_kernel,
        out_shape=jax.ShapeDtypeStruct((len(projs), batch, per_res, d, 2, HEADS_PER_GROUP, HEAD_DIM), F32),
        grid=(len(projs), batch, 2, d),
        in_specs=[pl.BlockSpec((per_res, ATTN_WIDTH), in_rows)] * 2,
        out_specs=pl.BlockSpec((None, None, per_res, None, None, HEADS_PER_GROUP, HEAD_DIM),
                               lambda layer, b, kv, r: (layer, b, 0, r, kv, 0, 0)),
        compiler_params=_params(("parallel",) * 4),
        name="kv_tails",
    )(*projs)
    return out.reshape(len(projs), batch, keep, 2, HEADS_PER_GROUP, HEAD_DIM)


def _set_rows_kernel(new_ref, cache_ref, o_ref):
    del cache_ref
    o_ref[...] = new_ref[...]


def _cache_set_rows(cache, new_rows, layer):
    batch, n_new = new_rows.shape[:2]
    tile = new_rows.shape[2:]
    zeros = (0,) * len(tile)
    last_block = cache.shape[2] // n_new - 1
    assert cache.shape[2] % n_new == 0
    return pl.pallas_call(
        _set_rows_kernel,
        out_shape=jax.ShapeDtypeStruct(cache.shape, cache.dtype),
        grid=(batch,),
        in_specs=[pl.BlockSpec((None, n_new) + tile, lambda b: (b, 0) + zeros),
                  pl.BlockSpec(memory_space=pl.ANY)],
        out_specs=pl.BlockSpec((None, None, n_new) + tile, lambda b: (layer, b, last_block) + zeros),
        input_output_aliases={1: 0},
        compiler_params=_params(("parallel",)),
        name="cache_set_rows",
    )(new_rows, cache)


def _scan_rows(a, b, h_prev):
    n_tiles = a.shape[0] // SUBLANES
    a3 = a.reshape(n_tiles, SUBLANES, a.shape[1])
    b3 = b.reshape(n_tiles, SUBLANES, b.shape[1])
    row = lax.broadcasted_iota(jnp.int32, a3.shape, 1)
    shift = 1
    while shift < SUBLANES:
        a_sh = pltpu.roll(a3, shift, axis=1)
        b_sh = pltpu.roll(b3, shift, axis=1)
        keep = row >= shift
        b3 = jnp.where(keep, a3 * b_sh + b3, b3)
        a3 = jnp.where(keep, a3 * a_sh, a3)
        shift *= 2
    tiles = []
    carry = h_prev
    for k in range(n_tiles):
        h_k = b3[k] + a3[k] * carry
        carry = h_k[SUBLANES - 1:SUBLANES]
        tiles.append(h_k)
    return jnp.concatenate(tiles, axis=0), carry


def _rnn_kernel(*refs, copy_window):
    (x_ref, ng_ref, win_ref, conv0_ref, h0_ref, cw_ref, cb_ref, gaw_ref, gab_ref,
     gxw_ref, gxb_ref, lam_ref, wout_ref), refs = refs[:13], refs[13:]
    if copy_window:
        (cache_hbm, newrows_ref, y_ref, hlast_ref, clast_ref, out_hbm, xe_scr, gate_scr, h_scr, u_scr,
         buf, rows_scr, in_sems, out_sems, row_sem) = refs
    else:
        y_ref, hlast_ref, clast_ref, xe_scr, gate_scr, h_scr, u_scr = refs
    c = pl.program_id(1)
    tl = x_ref.shape[0]
    pad = SUBLANES

    chunks, row_copy = [], None
    if copy_window:
        n_new = newrows_ref.shape[0]
        step = pl.program_id(0) * pl.num_programs(1) + c
        cache_layer, cache_row = step // cache_hbm.shape[1], step % cache_hbm.shape[1]
        chunks = _shift_copy_chunks(cache_hbm, out_hbm, cache_layer, cache_row, copy_window, n_new, buf, in_sems,
                                    out_sems, first_slot=0, keep_newest=False)
        assert len(chunks) <= buf.shape[0]
        rows_scr[...] = newrows_ref[...]
        row_copy = pltpu.make_async_copy(rows_scr, out_hbm.at[cache_layer, cache_row, pl.ds(copy_window - n_new, n_new)],
                                         row_sem.at[0])
        for cp_in, _ in chunks:
            cp_in.start()
        row_copy.start()

    @pl.when(c == 0)
    def _():
        xe_scr[pad - (CONV_W - 1):pad] = conv0_ref[...]
        h_scr[...] = h0_ref[...]

    @pl.when(c > 0)
    def _():
        xe_scr[0:pad] = xe_scr[tl:tl + pad]

    x = x_ref[...]
    ms = jnp.mean(x * x, axis=-1, keepdims=True)
    hn = (x * lax.rsqrt(ms + EPS) * ng_ref[...]).astype(BF16)
    xe_scr[pad:pad + tl] = jnp.dot(hn, win_ref[:, :D_RNN], preferred_element_type=F32)
    gate_scr[...] = jnp.dot(hn, win_ref[:, D_RNN:], preferred_element_type=F32)
    clast_ref[...] = xe_scr[pad + tl - (CONV_W - 1):pad + tl]

    for j in range(RNN_BLOCKS):
        cols = slice(j * RNN_BLOCK_W, (j + 1) * RNN_BLOCK_W)
        xc = cb_ref[:, cols]
        for k in range(CONV_W):
            start = pad - (CONV_W - 1) + k
            xc = xc + cw_ref[k:k + 1, cols] * xe_scr[start:start + tl, cols]
        xcb = xc.astype(BF16)
        r = jax.nn.sigmoid(jnp.dot(xcb, gaw_ref[j], preferred_element_type=F32) + gab_ref[:, cols])
        ig = jax.nn.sigmoid(jnp.dot(xcb, gxw_ref[j], preferred_element_type=F32) + gxb_ref[:, cols])
        neg_lam = -lam_ref[:, cols]
        softplus = jnp.maximum(neg_lam, 0.0) + jnp.log1p(jnp.exp(-jnp.abs(neg_lam)))
        log_a = -LRU_C * r * softplus
        a = jnp.exp(log_a)
        bterm = jnp.sqrt(-jnp.tanh(log_a) * (a * a + 1.0)) * (ig * xc)
        h, h_end = _scan_rows(a, bterm, h_scr[:, cols])
        h_scr[:, cols] = h_end
        gate = gate_scr[:, cols]
        u_scr[:, cols] = gate * jax.nn.sigmoid(gate) * h
        if j == RNN_BLOCKS // 2:
            for cp_in, cp_out in chunks:
                cp_in.wait()
                cp_out.start()

    hlast_ref[...] = h_scr[...]
    y_ref[...] = x_ref[...] + jnp.dot(u_scr[...].astype(BF16), wout_ref[...], preferred_element_type=F32)
    for _, cp_out in chunks:
        cp_out.wait()
    if row_copy is not None:
        row_copy.wait()


def _rnn_layer(x, norm_g, w_in, conv0, h0, conv_w, conv_b, ga_w, ga_b, gx_w, gx_b, lam, w_out, batch, seq, tl,
               cache_update=None):
    n_chunks = seq // tl
    assert seq % tl == 0 and tl % SUBLANES == 0
    row = lambda b, c: (b * n_chunks + c, 0)
    vec = lambda a: a.reshape(1, -1)
    full = lambda shape: pl.BlockSpec(shape, lambda b, c: (0,) * len(shape))
    per_batch = lambda rows_: pl.BlockSpec((None, rows_, D_RNN), lambda b, c: (b, 0, 0))
    any_spec = pl.BlockSpec(memory_space=pl.ANY)
    in_specs = [
        pl.BlockSpec((tl, D_MODEL), row),
        full((1, D_MODEL)),
        full((D_MODEL, 2 * D_RNN)),
        per_batch(CONV_W - 1),
        per_batch(1),
        full((CONV_W, D_RNN)),
        full((1, D_RNN)),
        full((RNN_BLOCKS, RNN_BLOCK_W, RNN_BLOCK_W)),
        full((1, D_RNN)),
        full((RNN_BLOCKS, RNN_BLOCK_W, RNN_BLOCK_W)),
        full((1, D_RNN)),
        full((1, D_RNN)),
        full((D_RNN, D_MODEL)),
    ]
    args = [x, vec(norm_g), w_in, conv0, h0.reshape(batch, 1, D_RNN), conv_w, vec(conv_b), ga_w, vec(ga_b),
            gx_w, vec(gx_b), vec(lam), w_out]
    out_shape = [jax.ShapeDtypeStruct((batch * seq, D_MODEL), F32),
                 jax.ShapeDtypeStruct((batch, 1, D_RNN), F32),
                 jax.ShapeDtypeStruct((batch, CONV_W - 1, D_RNN), F32)]
    out_specs = [pl.BlockSpec((tl, D_MODEL), row), per_batch(1), per_batch(CONV_W - 1)]
    scratch_shapes = [
        pltpu.VMEM((tl + SUBLANES, D_RNN), F32),
        pltpu.VMEM((tl, D_RNN), F32),
        pltpu.VMEM((1, D_RNN), F32),
        pltpu.VMEM((tl, D_RNN), F32),
    ]
    copy_window = 0
    if cache_update is not None:
        cache, new_rows = cache_update
        n_layers, cache_rows, copy_window = cache.shape[:3]
        n_new, tile = new_rows.shape[2], new_rows.shape[3:]
        n_slots = pl.cdiv(copy_window - n_new, COPY_CHUNK_ROWS)
        assert batch * n_chunks == n_layers * cache_rows
        in_specs += [any_spec, pl.BlockSpec(
            (None, None, n_new) + tile,
            lambda b, c: ((b * n_chunks + c) // cache_rows, (b * n_chunks + c) % cache_rows) + (0,) * (1 + len(tile)))]
        args += [cache, new_rows]
        out_shape.append(jax.ShapeDtypeStruct(cache.shape, cache.dtype))
        out_specs.append(any_spec)
        scratch_shapes += [
            pltpu.VMEM((n_slots, COPY_CHUNK_ROWS) + tile, cache.dtype),
            pltpu.VMEM((n_new,) + tile, cache.dtype),
            pltpu.SemaphoreType.DMA((n_slots,)),
            pltpu.SemaphoreType.DMA((n_slots,)),
            pltpu.SemaphoreType.DMA((1,)),
        ]
    res = pl.pallas_call(
        functools.partial(_rnn_kernel, copy_window=copy_window),
        out_shape=out_shape,
        grid=(batch, n_chunks),
        in_specs=in_specs,
        out_specs=out_specs,
        scratch_shapes=scratch_shapes,
        compiler_params=_params(("parallel", "arbitrary")),
        name="rnn_layer",
    )(*args)
    return (res[0], res[1].reshape(batch, D_RNN)) + tuple(res[2:])


def _t5_bucket(dist):
    n = jnp.maximum(dist, 0)
    max_exact = N_BUCKETS // 2
    nf = jnp.maximum(n, 1).astype(F32)
    large = max_exact + (jnp.log(nf / max_exact) / math.log(MAX_DISTANCE / max_exact)
                         * (N_BUCKETS - max_exact)).astype(jnp.int32)
    large = jnp.minimum(large, N_BUCKETS - 1)
    return jnp.where(n < max_exact, n, large)


def _bias_tables(rel_bias):
    steps = jnp.arange(N_KEYS + 1)
    period = 3 * Q_BLK
    bands, decs = [], []
    for g in range(N_GROUPS):
        heads = slice(g * HEADS_PER_GROUP, (g + 1) * HEADS_PER_GROUP)
        per_step = rel_bias[_t5_bucket(steps * DILATIONS[g])][:, heads]
        row = jnp.concatenate([per_step[::-1].T,
                               jnp.full((HEADS_PER_GROUP, period - (N_KEYS + 1)), NEG, F32)], axis=1)
        band = jnp.tile(row, (1, Q_BLK))[:, :Q_BLK * (period - 1)]
        bands.append(band.reshape(HEADS_PER_GROUP, Q_BLK, period - 1)[:, :, :2 * Q_BLK])
        decs.append(jnp.broadcast_to(per_step[::-1][:, :, None],
                                     (N_KEYS + 1, HEADS_PER_GROUP, HEAD_DIM)))
    return jnp.stack(bands, axis=1).astype(F32), jnp.stack(decs).astype(F32)


def kernel(x_prompt, x_sample, cache_kv_w128, cache_kv_w512, cache_kv_w2048, state_rglru_h,
           state_rglru_conv, attn_norm, attn_w_in, attn_q_norm, attn_k_norm, attn_w_out, rel_bias,
           rnn_norm, rnn_w_in, rnn_conv_w, rnn_conv_b, rnn_gate_a_w, rnn_gate_a_b, rnn_gate_x_w,
           rnn_gate_x_b, rnn_lambda, rnn_w_out):
    batch, seq, _ = x_prompt.shape
    bd, n_t, _ = x_sample.shape
    depth = attn_norm.shape[0] + rnn_norm.shape[0]
    caches = (cache_kv_w128, cache_kv_w512, cache_kv_w2048)

    yp = x_prompt.reshape(batch * seq, D_MODEL)
    ys = x_sample.reshape(bd * n_t, D_MODEL)
    tm_p, tm_s = OUT_ROWS, bd * n_t

    bias_band, bias_dec = _bias_tables(rel_bias)
    projs_p = []
    assert depth == 4
    h_p, h_s, c_p, c_s = [], [], [], []
    for i in range(depth):
        li = i // 2
        if i % 2 == 0:
            w_in = attn_w_in[li].astype(BF16)
            w_out = attn_w_out[li].astype(BF16)
            head_scale = jnp.concatenate([
                jnp.tile(attn_q_norm[li] * (HEAD_DIM ** -0.5), N_HEADS),
                jnp.tile(attn_k_norm[li], N_HEADS)]).reshape(1, 2 * QKV_WIDTH)
            proj_p = _norm_proj(yp, attn_norm[li], w_in, head_scale, SPAN, PROJ_COLS, row_orders=DILATIONS)
            proj_s = _norm_proj(ys, attn_norm[li], w_in, head_scale, tm_s, PROJ_COLS, split_heads=True)
            projs_p.append(proj_p)

            u_p = _attn_prompt(proj_p, bias_band, batch, seq)
            yp = _out_proj(u_p, w_out, yp, tm_p)

            proj_s = proj_s.reshape(bd, n_t, -1, HEAD_DIM)
            if li == 0:
                u_s, new_rows0, (cache_w128, cache_w512) = _decode_attn(li, proj_s, bias_dec, caches, (0, 1))
            else:
                u_s, new_rows1, _ = _decode_attn(li, proj_s, bias_dec, caches, ())
                cache_w128 = _cache_set_rows(cache_w128, new_rows1[0], li)
                cache_w512 = _cache_set_rows(cache_w512, new_rows1[1], li)
            ys = _out_proj(u_s.reshape(bd * n_t, ATTN_WIDTH), w_out, ys, tm_s)
        else:
            w_in = rnn_w_in[li].astype(BF16)
            rnn_args = (rnn_conv_w[li], rnn_conv_b[li], rnn_gate_a_w[li].astype(BF16), rnn_gate_a_b[li].reshape(-1),
                        rnn_gate_x_w[li].astype(BF16), rnn_gate_x_b[li].reshape(-1), rnn_lambda[li],
                        rnn_w_out[li].astype(BF16))
            big_window = None if li == 0 else (caches[-1], jnp.stack([new_rows0[-1], new_rows1[-1]]))
            yp, hp, cp, *updated = _rnn_layer(yp, rnn_norm[li], w_in, jnp.zeros((batch, CONV_W - 1, D_RNN), F32),
                                              jnp.zeros((batch, D_RNN), F32), *rnn_args, batch, seq, RNN_CHUNK,
                                              cache_update=big_window)
            if updated:
                cache_w2048, = updated
            ys, hs, cs = _rnn_layer(ys, rnn_norm[li], w_in, state_rglru_conv[li], state_rglru_h[li], *rnn_args,
                                    bd, n_t, n_t)
            h_p.append(hp)
            h_s.append(hs)
            c_p.append(cp)
            c_s.append(cs)

    kv_p = [_kv_tails(projs_p, g, batch, seq) for g in range(N_GROUPS)]
    return (yp.reshape(batch, seq, D_MODEL), ys.reshape(bd, n_t, D_MODEL),
            kv_p[0], cache_w128,
            kv_p[1], cache_w512,
            kv_p[2], cache_w2048,
            jnp.stack(h_p), jnp.stack(h_s),
            jnp.stack(c_p), jnp.stack(c_s))
```
